```python
import math
import jax
import jax.numpy as jnp
from jax import lax
import numpy as np

D_MODEL = 2048
BATCH = 1
SEQ = 8192
DEPTH = 2
DEC_BATCH = 32
DEC_SEQ = 1
PAST_LEN = 8192
PAGE_SIZE = 128

N_AB_LAYERS = (DEPTH + 1) // 2
N_C_LAYERS = DEPTH // 2
NORM_EPS = 1e-5
N_MOD = 6
A_HEADS = 8
A_QK_DIM = 64
A_V_DIM = 2 * A_QK_DIM
QK_WIDTH = A_HEADS * 2 * A_QK_DIM
A_WIDTH = A_HEADS * A_V_DIM
A_SCALE = A_QK_DIM ** -0.5
ROPE_THETA = 10000.0
Q_BLOCK = 128
POOL_WINDOWS = (2, 4, 8, 16)
POOL_GROUPS = 4
POOL_WIDTH = D_MODEL // 2
POOL_GROUP_DIM = POOL_WIDTH // POOL_GROUPS
POOL_HIST = 15
AB_IN_WIDTH = 2 * QK_WIDTH + A_WIDTH + POOL_WIDTH
SSM_INNER = 2 * D_MODEL
SSM_HEAD_DIM = 64
SSM_HEADS = SSM_INNER // SSM_HEAD_DIM
SSM_GROUPS = 8
SSM_HPG = SSM_HEADS // SSM_GROUPS
SSM_STATE = 128
SSM_CONV = 4
SSM_CHUNK = 128
SSM_CONV_DIM = SSM_INNER + 2 * SSM_GROUPS * SSM_STATE
SSM_IN_WIDTH = SSM_INNER + SSM_CONV_DIM + SSM_HEADS
N_EXPERTS = 32
TOP_K = 4
D_FF_EXPERT = D_MODEL
SWIGLU_ALPHA = 1.702
SWIGLU_LIMIT = 7.0

kernel_name = "hybrid_diffattn_pool_ssd_moe_step"


def rmsnorm(x, eps=NORM_EPS):
    x32 = x.astype(jnp.float32)
    return (x32 * lax.rsqrt(jnp.mean(x32 * x32, axis=-1, keepdims=True) + eps)).astype(x.dtype)


def lambda_init(layer):
    return 0.8 - 0.6 * math.exp(-0.3 * layer)


def rope(x, pos):
    d = x.shape[-1]
    half = d // 2
    inv = ROPE_THETA ** (-jnp.arange(half, dtype=jnp.float32) * 2.0 / d)
    ang = pos.astype(jnp.float32)[:, None] * inv[None, :]
    cos = jnp.cos(ang)[None, :, None, None, :]
    sin = jnp.sin(ang)[None, :, None, None, :]
    x32 = x.astype(jnp.float32)
    x1, x2 = x32[..., :half], x32[..., half:]
    return jnp.concatenate([x1 * cos - x2 * sin, x2 * cos + x1 * sin], axis=-1).astype(x.dtype)


def diff_attend(q, k_segs, v_segs, q_pos, k_pos, lam):
    s = jnp.concatenate([jnp.einsum('bqhcd,bkhcd->bhcqk', q, kk) for kk in k_segs], axis=-1)
    s = s.astype(jnp.float32) * A_SCALE
    s = jnp.where(k_pos[None, :] <= q_pos[:, None], s, -jnp.inf)
    p = jax.nn.softmax(s, axis=-1)
    a = (p[:, :, 0] - lam * p[:, :, 1]).astype(q.dtype)
    outs = []
    off = 0
    for vv in v_segs:
        n = vv.shape[1]
        outs.append(jnp.einsum('bhqk,bkhe->bqhe', a[..., off:off + n], vv))
        off += n
    return sum(outs)


def diff_attention_blocked(q, k, v, pos, lam):
    b, l = q.shape[:2]
    nb = l // Q_BLOCK
    qb = jnp.moveaxis(q.reshape(b, nb, Q_BLOCK, A_HEADS, 2, A_QK_DIM), 1, 0)
    pb = pos.reshape(nb, Q_BLOCK)
    ob = lax.map(lambda a: diff_attend(a[0], (k,), (v,), a[1], pos, lam), (qb, pb))
    return jnp.moveaxis(ob, 0, 1).reshape(b, l, A_HEADS, A_V_DIM)


def multiscale_pool(u_ext, pos):
    b, _, c = u_ext.shape
    l = pos.shape[0]
    cs = jnp.cumsum(u_ext.astype(jnp.float32), axis=1)
    cs = jnp.concatenate([jnp.zeros((b, 1, c), jnp.float32), cs], axis=1)
    end = cs[:, POOL_HIST + 1:]
    outs = []
    for g, w in enumerate(POOL_WINDOWS):
        sl = slice(g * POOL_GROUP_DIM, (g + 1) * POOL_GROUP_DIM)
        start = cs[:, POOL_HIST + 1 - w:POOL_HIST + 1 - w + l, sl]
        cnt = jnp.minimum(w, pos + 1).astype(jnp.float32)[None, :, None]
        outs.append((end[..., sl] - start) / cnt)
    mean = jnp.concatenate(outs, axis=-1)
    return (mean - u_ext[:, POOL_HIST:].astype(jnp.float32)).astype(u_ext.dtype)


def mixer_ab(h, pos, pool_hist, kv_past, w_in, lam_vec, subln_w, pool_w, pool_scale, w_out, lam_init):
    b, l, _ = h.shape
    q, k, v, u = jnp.split(h @ w_in, [QK_WIDTH, 2 * QK_WIDTH, 2 * QK_WIDTH + A_WIDTH], axis=-1)
    q = rope(q.reshape(b, l, A_HEADS, 2, A_QK_DIM), pos)
    k = rope(k.reshape(b, l, A_HEADS, 2, A_QK_DIM), pos)
    v = v.reshape(b, l, A_HEADS, A_V_DIM)
    lv = lam_vec.astype(jnp.float32)
    lam = jnp.exp(jnp.sum(lv[0] * lv[1])) - jnp.exp(jnp.sum(lv[2] * lv[3])) + lam_init
    if kv_past is None:
        o = diff_attention_blocked(q, k, v, pos, lam)
    else:
        k_past, v_past = kv_past
        k_pos = jnp.concatenate([jnp.arange(k_past.shape[1]), pos])
        o = diff_attend(q, (k_past, k), (v_past, v), pos, k_pos, lam)
    o = rmsnorm(o) * subln_w * (1.0 - lam_init)
    u_ext = jnp.concatenate([pool_hist.astype(u.dtype), u], axis=1)
    pooled = multiscale_pool(u_ext, pos)
    pg = jnp.einsum('blgc,gce->blge', pooled.reshape(b, l, POOL_GROUPS, POOL_GROUP_DIM), pool_w)
    pg = pg.reshape(b, l, POOL_WIDTH) * pool_scale
    out = jnp.concatenate([o.reshape(b, l, A_WIDTH).astype(pg.dtype), pg], axis=-1) @ w_out
    return out, k, v, u_ext[:, -POOL_HIST:]


def ssd_chunked(x, dt, a, bm, cm):
    b, l, g, r, p = x.shape
    n = bm.shape[-1]
    nc = l // SSM_CHUNK
    x = x.reshape(b, nc, SSM_CHUNK, g, r, p)
    dt = dt.reshape(b, nc, SSM_CHUNK, g, r)
    bm = bm.reshape(b, nc, SSM_CHUNK, g, n)
    cm = cm.reshape(b, nc, SSM_CHUNK, g, n)
    cum = jnp.cumsum(dt * a, axis=2)
    seg = cum[:, :, :, None] - cum[:, :, None, :]
    causal = jnp.tril(jnp.ones((SSM_CHUNK, SSM_CHUNK), bool))[:, :, None, None]
    lmat = jnp.exp(jnp.where(causal, seg, -jnp.inf))
    cb = jnp.einsum('bcign,bcjgn->bcijg', cm, bm)
    xdt = x * dt[..., None]
    y_diag = jnp.einsum('bcijgr,bcjgrp->bcigrp', cb[..., None] * lmat, xdt)
    decay_end = jnp.exp(cum[:, :, -1:] - cum)
    states = jnp.einsum('bcjgn,bcjgrp->bcgrpn', bm, xdt * decay_end[..., None])
    chunk_decay = jnp.exp(cum[:, :, -1])

    def step(hc, inp):
        s_c, d_c = inp
        return hc * d_c[..., None, None] + s_c, hc

    h0 = jnp.zeros((b, g, r, p, n), jnp.float32)
    h_final, h_start = lax.scan(step, h0, (jnp.moveaxis(states, 1, 0), jnp.moveaxis(chunk_decay, 1, 0)))
    h_start = jnp.moveaxis(h_start, 0, 1)
    y_off = jnp.einsum('bcign,bcgrpn->bcigrp', cm, h_start) * jnp.exp(cum)[..., None]
    return (y_diag + y_off).reshape(b, l, g, r, p), h_final


def ssd_recurrent(x, dt, a, bm, cm, h0):
    def step(hs, inp):
        x_t, dt_t, b_t, c_t = inp
        hs = hs * jnp.exp(dt_t * a)[..., None, None] + jnp.einsum('bgrp,bgn->bgrpn', x_t * dt_t[..., None], b_t)
        return hs, jnp.einsum('bgn,bgrpn->bgrp', c_t, hs)

    hf, ys = lax.scan(step, h0, (jnp.moveaxis(x, 1, 0), jnp.moveaxis(dt, 1, 0), jnp.moveaxis(bm, 1, 0), jnp.moveaxis(cm, 1, 0)))
    return jnp.moveaxis(ys, 0, 1), hf


def mixer_ssd(h, conv_hist, h0, w_in, conv_w, conv_b, dt_bias, a_log, d_skip, norm_w, w_out):
    b, l, _ = h.shape
    z, xbc, dt = jnp.split(h @ w_in, [SSM_INNER, SSM_INNER + SSM_CONV_DIM], axis=-1)
    xbc_ext = jnp.concatenate([conv_hist.astype(xbc.dtype), xbc], axis=1)
    conv = conv_b + sum(xbc_ext[:, t:t + l] * conv_w[t] for t in range(SSM_CONV))
    xbc_c = jax.nn.silu(conv)
    xs, bm, cm = jnp.split(xbc_c, [SSM_INNER, SSM_INNER + SSM_GROUPS * SSM_STATE], axis=-1)
    xs = xs.reshape(b, l, SSM_GROUPS, SSM_HPG, SSM_HEAD_DIM).astype(jnp.float32)
    bm = bm.reshape(b, l, SSM_GROUPS, SSM_STATE).astype(jnp.float32)
    cm = cm.reshape(b, l, SSM_GROUPS, SSM_STATE).astype(jnp.float32)
    dt = jax.nn.softplus(dt.astype(jnp.float32) + dt_bias.astype(jnp.float32)).reshape(b, l, SSM_GROUPS, SSM_HPG)
    a = -jnp.exp(a_log.astype(jnp.float32)).reshape(SSM_GROUPS, SSM_HPG)
    if h0 is None:
        y, hf = ssd_chunked(xs, dt, a, bm, cm)
    else:
        hs0 = h0.astype(jnp.float32).reshape(b, SSM_GROUPS, SSM_HPG, SSM_HEAD_DIM, SSM_STATE)
        y, hf = ssd_recurrent(xs, dt, a, bm, cm, hs0)
    y = y + d_skip.astype(jnp.float32).reshape(SSM_GROUPS, SSM_HPG, 1) * xs
    y = y.reshape(b, l, SSM_INNER) * jax.nn.silu(z.astype(jnp.float32))
    y = rmsnorm(y.reshape(b, l, SSM_GROUPS, SSM_INNER // SSM_GROUPS)).reshape(b, l, SSM_INNER) * norm_w
    out = y.astype(h.dtype) @ w_out
    return out, xbc_ext[:, -(SSM_CONV - 1):], hf.reshape(b, SSM_HEADS, SSM_HEAD_DIM, SSM_STATE).astype(h.dtype)


def moe(h, w_r, b_r, w_gu, b_gu, w_dn, b_dn):
    shp = h.shape
    t = h.reshape(-1, shp[-1])
    logits = (t @ w_r + b_r).astype(jnp.float32)
    top_v, top_i = lax.top_k(logits, TOP_K)
    top_p = jax.nn.softmax(top_v, axis=-1)
    gates = jnp.sum(jax.nn.one_hot(top_i, N_EXPERTS, dtype=jnp.float32) * top_p[..., None], axis=1)

    def expert(acc, e_in):
        wgu, bgu, wdn, bdn, g = e_in
        gu = t @ wgu + bgu
        gate = jnp.minimum(gu[:, ::2], SWIGLU_LIMIT)
        up = jnp.clip(gu[:, 1::2], -SWIGLU_LIMIT, SWIGLU_LIMIT)
        act = (up + 1.0) * gate * jax.nn.sigmoid(SWIGLU_ALPHA * gate)
        return acc + g[:, None] * (act @ wdn + bdn).astype(jnp.float32), None

    acc, _ = lax.scan(expert, jnp.zeros((t.shape[0], shp[-1]), jnp.float32), (w_gu, b_gu, w_dn, b_dn, gates.T))
    return acc.astype(h.dtype).reshape(shp)


def setup_inputs(seed: int = 0) -> dict:
    key = jax.random.key(seed)
    keys = list(jax.random.split(key, 40))

    def nrm(shape, scale=1.0):
        return jax.random.normal(keys.pop(), shape, jnp.float32) * scale

    def gain(shape):
        return 1.0 + nrm(shape, 0.05)

    n_pages = PAST_LEN // PAGE_SIZE
    n_phys = (DEC_BATCH * n_pages * 5 + 3) // 4
    page_table = jax.random.permutation(keys.pop(), n_phys)[:DEC_BATCH * n_pages].reshape(DEC_BATCH, n_pages).astype(jnp.int32)
    dt0 = jnp.exp(jax.random.uniform(keys.pop(), (N_C_LAYERS, SSM_HEADS), jnp.float32, math.log(1e-3), math.log(1e-1)))
    dt_bias = dt0 + jnp.log(-jnp.expm1(-dt0))
    a_log = jnp.log(jax.random.uniform(keys.pop(), (N_C_LAYERS, SSM_HEADS), jnp.float32, 1.0, 16.0))
    d = D_MODEL ** -0.5
    return {
        'x_prompt': nrm((BATCH, SEQ, D_MODEL)),
        'x_sample': nrm((DEC_BATCH, DEC_SEQ, D_MODEL)),
        'cache_k': nrm((N_AB_LAYERS, n_phys, PAGE_SIZE, A_HEADS, 2, A_QK_DIM)),
        'cache_v': nrm((N_AB_LAYERS, n_phys, PAGE_SIZE, A_HEADS, A_V_DIM)),
        'state_pool': nrm((N_AB_LAYERS, DEC_BATCH, POOL_HIST, POOL_WIDTH)),
        'state_conv': nrm((N_C_LAYERS, DEC_BATCH, SSM_CONV - 1, SSM_CONV_DIM)),
        'state_ssm': nrm((N_C_LAYERS, DEC_BATCH, SSM_HEADS, SSM_HEAD_DIM, SSM_STATE), 0.1),
        'page_table': page_table,
        'c_prompt': nrm((BATCH, D_MODEL)),
        'c_sample': nrm((DEC_BATCH, D_MODEL)),
        'ada_w': nrm((DEPTH, D_MODEL, N_MOD * D_MODEL), 0.5 * d),
        'ada_b': nrm((DEPTH, N_MOD * D_MODEL), 0.02),
        'ab_w_in': nrm((N_AB_LAYERS, D_MODEL, AB_IN_WIDTH), d),
        'ab_lambda': nrm((N_AB_LAYERS, 4, A_QK_DIM), 0.1),
        'ab_subln': gain((N_AB_LAYERS, A_V_DIM)),
        'ab_pool_w': nrm((N_AB_LAYERS, POOL_GROUPS, POOL_GROUP_DIM, POOL_GROUP_DIM), POOL_GROUP_DIM ** -0.5),
        'ab_pool_scale': gain((N_AB_LAYERS, POOL_WIDTH)),
        'ab_w_out': nrm((N_AB_LAYERS, A_WIDTH + POOL_WIDTH, D_MODEL), (A_WIDTH + POOL_WIDTH) ** -0.5),
        'ssd_w_in': nrm((N_C_LAYERS, D_MODEL, SSM_IN_WIDTH), d),
        'ssd_conv_w': nrm((N_C_LAYERS, SSM_CONV, SSM_CONV_DIM), 0.5),
        'ssd_conv_b': nrm((N_C_LAYERS, SSM_CONV_DIM), 0.02),
        'ssd_dt_bias': dt_bias,
        'ssd_a_log': a_log,
        'ssd_d': gain((N_C_LAYERS, SSM_HEADS)),
        'ssd_norm': gain((N_C_LAYERS, SSM_INNER)),
        'ssd_w_out': nrm((N_C_LAYERS, SSM_INNER, D_MODEL), SSM_INNER ** -0.5),
        'router_w': nrm((DEPTH, D_MODEL, N_EXPERTS), d),
        'router_b': nrm((DEPTH, N_EXPERTS), 0.01),
        'moe_w_gu': nrm((DEPTH, N_EXPERTS, D_MODEL, 2 * D_FF_EXPERT), d),
        'moe_b_gu': nrm((DEPTH, N_EXPERTS, 2 * D_FF_EXPERT), 0.02),
        'moe_w_dn': nrm((DEPTH, N_EXPERTS, D_FF_EXPERT, D_MODEL), D_FF_EXPERT ** -0.5),
        'moe_b_dn': nrm((DEPTH, N_EXPERTS, D_MODEL), 0.02),
        'final_norm': gain((D_MODEL,)),
    }


def reference(x_prompt, x_sample, cache_k, cache_v, state_pool, state_conv, state_ssm, page_table, c_prompt, c_sample,
              ada_w, ada_b, ab_w_in, ab_lambda, ab_subln, ab_pool_w, ab_pool_scale, ab_w_out,
              ssd_w_in, ssd_conv_w, ssd_conv_b, ssd_dt_bias, ssd_a_log, ssd_d, ssd_norm, ssd_w_out,
              router_w, router_b, moe_w_gu, moe_b_gu, moe_w_dn, moe_b_dn, final_norm):
    past_len = page_table.shape[1] * cache_k.shape[2]

    def trunk(x, c, pos, sample):
        b = x.shape[0]
        k_rows, v_rows, pools, convs, ssms = [], [], [], [], []
        for i in range(DEPTH):
            mod = (jax.nn.silu(c) @ ada_w[i] + ada_b[i]).reshape(b, N_MOD, 1, D_MODEL)
            shift1, scale1, gate1, shift2, scale2, gate2 = (mod[:, m] for m in range(N_MOD))
            hn = rmsnorm(x) * (1.0 + scale1) + shift1
            j = i // 2
            if i % 2 == 0:
                if sample:
                    k_past = cache_k[j, page_table].reshape(b, past_len, A_HEADS, 2, A_QK_DIM)
                    v_past = cache_v[j, page_table].reshape(b, past_len, A_HEADS, A_V_DIM)
                    kv, ph = (k_past, v_past), state_pool[j]
                else:
                    kv, ph = None, jnp.zeros((b, POOL_HIST, POOL_WIDTH), x.dtype)
                out, kn, vn, pn = mixer_ab(hn, pos, ph, kv, ab_w_in[j], ab_lambda[j], ab_subln[j], ab_pool_w[j],
                                           ab_pool_scale[j], ab_w_out[j], lambda_init(i))
                k_rows.append(kn)
                v_rows.append(vn)
                pools.append(pn)
            else:
                if sample:
                    ch, h0 = state_conv[j], state_ssm[j]
                else:
                    ch, h0 = jnp.zeros((b, SSM_CONV - 1, SSM_CONV_DIM), x.dtype), None
                out, cn, sn = mixer_ssd(hn, ch, h0, ssd_w_in[j], ssd_conv_w[j], ssd_conv_b[j], ssd_dt_bias[j],
                                        ssd_a_log[j], ssd_d[j], ssd_norm[j], ssd_w_out[j])
                convs.append(cn)
                ssms.append(sn)
            x = x + gate1 * out
            hn = rmsnorm(x) * (1.0 + scale2) + shift2
            x = x + gate2 * moe(hn, router_w[i], router_b[i], moe_w_gu[i], moe_b_gu[i], moe_w_dn[i], moe_b_dn[i])
        y = rmsnorm(x) * final_norm
        return y, jnp.stack(k_rows), jnp.stack(v_rows), jnp.stack(pools), jnp.stack(convs), jnp.stack(ssms)

    pos_p = jnp.arange(x_prompt.shape[1])
    pos_s = past_len + jnp.arange(x_sample.shape[1])
    y_prompt, k_prompt, v_prompt, pool_prompt, conv_prompt, ssm_prompt = trunk(x_prompt, c_prompt, pos_p, False)
    y_sample, k_sample, v_sample, pool_sample, conv_sample, ssm_sample = trunk(x_sample, c_sample, pos_s, True)
    return (y_prompt, y_sample, k_prompt, v_prompt, pool_prompt, conv_prompt, ssm_prompt,
            k_sample, v_sample, pool_sample, conv_sample, ssm_sample)
```

```python
import functools
import math

import jax
import jax.numpy as jnp
from jax import lax
from jax.experimental import pallas as pl
from jax.experimental.pallas import tpu as pltpu

F32 = jnp.float32
BF16 = jnp.bfloat16
NEG_INF = float("-inf")

D_MODEL = 2048
DEPTH = 2
PAGE_SIZE = 128
NORM_EPS = 1e-5
N_MOD = 6
A_HEADS = 8
A_QK_DIM = 64
A_V_DIM = 128
QK_WIDTH = A_HEADS * 2 * A_QK_DIM
A_WIDTH = A_HEADS * A_V_DIM
A_SCALE = A_QK_DIM ** -0.5
ROPE_THETA = 10000.0
POOL_WINDOWS = (2, 4, 8, 16)
POOL_WIDTH = D_MODEL // 2
POOL_GROUP_DIM = POOL_WIDTH // 4
POOL_HIST = 15
SSM_INNER = 2 * D_MODEL
SSM_HEAD_DIM = 64
SSM_HEADS = SSM_INNER // SSM_HEAD_DIM
SSM_GROUPS = 8
SSM_HPG = SSM_HEADS // SSM_GROUPS
SSM_STATE = 128
SSM_CONV = 4
SSM_CHUNK = 128
SSM_GROUP_W = SSM_INNER // SSM_GROUPS
SSM_BC_W = SSM_GROUPS * SSM_STATE
SSM_CONV_DIM = SSM_INNER + 2 * SSM_BC_W
N_EXPERTS = 32
TOP_K = 4
SWIGLU_ALPHA = 1.702
SWIGLU_LIMIT = 7.0

LANES = 128
SUBLANES = 8
VMEM_LIMIT = 56 * 1024 * 1024

TM = 512
MOE_TM = 512
MOE_TN = 512
ATT_T = 512
DEC_PPS = 2


def _params(sem):
    return pltpu.CompilerParams(dimension_semantics=sem, vmem_limit_bytes=VMEM_LIMIT)


def _lambda_init(layer):
    return 0.8 - 0.6 * math.exp(-0.3 * layer)


def _split3(x):
    hi = x.astype(BF16)
    r1 = x - hi.astype(F32)
    mid = r1.astype(BF16)
    lo = (r1 - mid.astype(F32)).astype(BF16)
    return hi, mid, lo


def _dot(a, b):
    return jnp.dot(a, b, preferred_element_type=F32)


def _dot_nt(a, b):
    return lax.dot_general(a, b, (((1,), (1,)), ((), ())), preferred_element_type=F32)


def _dot_exact_rhs(x, sel):
    hi, mid, lo = _split3(x)
    return _dot(hi, sel) + _dot(mid, sel) + _dot(lo, sel)


def _silu(x):
    return x * jax.nn.sigmoid(x)


def _rms(x):
    return x * lax.rsqrt(jnp.mean(x * x, axis=-1, keepdims=True) + NORM_EPS)


def _ada_kernel(c_ref, w_ref, b_ref, o_ref):
    a = _silu(c_ref[...]).astype(BF16)
    o_ref[...] = _dot(a, w_ref[...].astype(BF16)) + b_ref[...]


def _ada_mod(c_all, ada_w, ada_b, tn=1024):
    m, d = c_all.shape
    nl, _, n = ada_w.shape
    return pl.pallas_call(
        _ada_kernel,
        grid=(nl, n // tn),
        in_specs=[
            pl.BlockSpec((m, d), lambda l, j: (0, 0)),
            pl.BlockSpec((None, d, tn), lambda l, j: (l, 0, j)),
            pl.BlockSpec((None, 1, tn), lambda l, j: (l, 0, j)),
        ],
        out_specs=pl.BlockSpec((None, m, tn), lambda l, j: (l, 0, j)),
        out_shape=jax.ShapeDtypeStruct((nl, m, n), F32),
        compiler_params=_params(("arbitrary", "arbitrary")),
        name="ada_mod",
    )(c_all, ada_w, ada_b.reshape(nl, 1, n))


def _norm_mm_kernel(*refs, rope_tiles, tn):
    if rope_tiles:
        x_ref, sc_ref, sh_ref, w_ref, cos_ref, sin_ref, o_ref, hn_ref = refs
    else:
        x_ref, sc_ref, sh_ref, w_ref, o_ref, hn_ref = refs
    j = pl.program_id(1)

    @pl.when(j == 0)
    def _():
        hn_ref[...] = (_rms(x_ref[...]) * (1.0 + sc_ref[...]) + sh_ref[...]).astype(BF16)

    y = _dot(hn_ref[...], w_ref[...])
    if not rope_tiles:
        o_ref[...] = y
        return

    @pl.when(j >= rope_tiles)
    def _():
        o_ref[...] = y

    @pl.when(j < rope_tiles)
    def _():
        cos = cos_ref[...]
        sin = sin_ref[...]
        lane = lax.broadcasted_iota(jnp.int32, cos.shape, 1)
        first = (lane % A_QK_DIM) < (A_QK_DIM // 2)
        for c in range(tn // LANES):
            yc = y[:, c * LANES:(c + 1) * LANES]
            partner = jnp.where(first, pltpu.roll(yc, LANES - A_QK_DIM // 2, 1),
                                pltpu.roll(yc, A_QK_DIM // 2, 1))
            o_ref[:, c * LANES:(c + 1) * LANES] = yc * cos + partner * sin


def _norm_matmul(x, scale, shift, w, tn, rope=None, rope_cols=0):
    m, d = x.shape
    n = w.shape[1]
    tm = min(TM, m)
    ms = scale.shape[0]
    mod_spec = pl.BlockSpec((tm if ms > 1 else 1, d), (lambda i, j: (i, 0)) if ms > 1 else (lambda i, j: (0, 0)))
    in_specs = [pl.BlockSpec((tm, d), lambda i, j: (i, 0)), mod_spec, mod_spec,
                pl.BlockSpec((d, tn), lambda i, j: (0, j))]
    args = [x, scale, shift, w]
    if rope is not None:
        in_specs += [pl.BlockSpec((tm, LANES), lambda i, j: (i, 0))] * 2
        args += list(rope)
    return pl.pallas_call(
        functools.partial(_norm_mm_kernel, rope_tiles=rope_cols // tn, tn=tn),
        grid=(m // tm, n // tn),
        in_specs=in_specs,
        out_specs=pl.BlockSpec((tm, tn), lambda i, j: (i, j)),
        out_shape=jax.ShapeDtypeStruct((m, n), F32),
        scratch_shapes=[pltpu.VMEM((tm, d), BF16)],
        compiler_params=_params(("arbitrary", "arbitrary")),
        name="norm_matmul",
    )(*args)


def _mm_resid_kernel(*refs, n_a):
    a_refs = refs[:n_a]
    w_ref, x_ref, g_ref, o_ref, ab_ref = refs[n_a:]

    @pl.when(pl.program_id(1) == 0)
    def _():
        off = 0
        for a_ref in a_refs:
            ka = a_ref.shape[1]
            ab_ref[:, off:off + ka] = a_ref[...].astype(BF16)
            off += ka

    o_ref[...] = x_ref[...] + g_ref[...] * _dot(ab_ref[...], w_ref[...])


def _matmul_resid(a_list, w, x, gate, tn=1024):
    m, n = x.shape
    k = w.shape[0]
    tm = min(TM, m)
    mg = gate.shape[0]
    in_specs = [pl.BlockSpec((tm, a.shape[1]), lambda i, j: (i, 0)) for a in a_list]
    in_specs += [
        pl.BlockSpec((k, tn), lambda i, j: (0, j)),
        pl.BlockSpec((tm, tn), lambda i, j: (i, j)),
        pl.BlockSpec((tm if mg > 1 else 1, tn), (lambda i, j: (i, j)) if mg > 1 else (lambda i, j: (0, j))),
    ]
    return pl.pallas_call(
        functools.partial(_mm_resid_kernel, n_a=len(a_list)),
        grid=(m // tm, n // tn),
        in_specs=in_specs,
        out_specs=pl.BlockSpec((tm, tn), lambda i, j: (i, j)),
        out_shape=jax.ShapeDtypeStruct((m, n), F32),
        scratch_shapes=[pltpu.VMEM((tm, k), BF16)],
        compiler_params=_params(("arbitrary", "arbitrary")),
        name="matmul_resid",
    )(*a_list, w, x, gate)


def _lam_value(lam_ref, lam_init):
    lv = lam_ref[...]
    s01 = jnp.sum(lv[0:1] * lv[1:2], axis=1, keepdims=True)
    s23 = jnp.sum(lv[2:3] * lv[3:4], axis=1, keepdims=True)
    return jnp.exp(s01) - jnp.exp(s23) + lam_init


def _attn_p_kernel(qi_ref, ki_ref, q_ref, k_ref, v_ref, lam_ref, sub_ref, o_ref, m_ref, l_ref, acc_ref, *, lam_init):
    p = pl.program_id(1)
    qi = qi_ref[p]
    ki = ki_ref[p]
    t = q_ref.shape[0]

    @pl.when(ki == 0)
    def _():
        m_ref[...] = jnp.full(m_ref.shape, NEG_INF, F32)
        l_ref[...] = jnp.zeros(l_ref.shape, F32)
        acc_ref[...] = jnp.zeros(acc_ref.shape, F32)

    def step(masked):
        q = q_ref[...] * A_SCALE
        lane = lax.broadcasted_iota(jnp.int32, q.shape, 1)
        kb = k_ref[...].astype(BF16)
        vb = v_ref[...].astype(BF16)
        if masked:
            row = lax.broadcasted_iota(jnp.int32, (t, t), 0)
            col = lax.broadcasted_iota(jnp.int32, (t, t), 1)
            keep = col <= row
        for c in range(2):
            qc = jnp.where((lane < A_QK_DIM) == (c == 0), q, 0.0).astype(BF16)
            s = _dot_nt(qc, kb)
            if masked:
                s = jnp.where(keep, s, NEG_INF)
            m_prev = m_ref[c]
            m_new = jnp.maximum(m_prev, jnp.max(s, axis=1, keepdims=True))
            alpha = jnp.exp(m_prev - m_new)
            e = jnp.exp(s - m_new)
            l_ref[c] = alpha * l_ref[c] + jnp.sum(e, axis=1, keepdims=True)
            acc_ref[c] = alpha * acc_ref[c] + _dot(e.astype(BF16), vb)
            m_ref[c] = m_new

    @pl.when(ki < qi)
    def _():
        step(False)

    @pl.when(ki == qi)
    def _():
        step(True)
        lam = _lam_value(lam_ref, lam_init)
        o = acc_ref[0] / l_ref[0] - lam * (acc_ref[1] / l_ref[1])
        o_ref[...] = _rms(o) * sub_ref[...] * (1.0 - lam_init)


def _attn_prompt(qkvu, lam_vec, subln, lam_init):
    m = qkvu.shape[0]
    t = min(ATT_T, m)
    nq = m // t
    pairs = [(qi, ki) for qi in range(nq) for ki in range(qi + 1)]
    qi_tab = jnp.asarray([p[0] for p in pairs], jnp.int32)
    ki_tab = jnp.asarray([p[1] for p in pairs], jnp.int32)
    kcol = QK_WIDTH // LANES
    vcol = 2 * QK_WIDTH // LANES
    grid_spec = pltpu.PrefetchScalarGridSpec(
        num_scalar_prefetch=2,
        grid=(A_HEADS, len(pairs)),
        in_specs=[
            pl.BlockSpec((t, LANES), lambda h, p, qt, kt: (qt[p], h)),
            pl.BlockSpec((t, LANES), lambda h, p, qt, kt: (kt[p], kcol + h)),
            pl.BlockSpec((t, LANES), lambda h, p, qt, kt: (kt[p], vcol + h)),
            pl.BlockSpec((4, A_QK_DIM), lambda h, p, qt, kt: (0, 0)),
            pl.BlockSpec((1, A_V_DIM), lambda h, p, qt, kt: (0, 0)),
        ],
        out_specs=pl.BlockSpec((t, LANES), lambda h, p, qt, kt: (qt[p], h)),
        scratch_shapes=[pltpu.VMEM((2, t, 1), F32), pltpu.VMEM((2, t, 1), F32), pltpu.VMEM((2, t, LANES), F32)],
    )
    return pl.pallas_call(
        functools.partial(_attn_p_kernel, lam_init=lam_init),
        grid_spec=grid_spec,
        out_shape=jax.ShapeDtypeStruct((m, A_WIDTH), F32),
        compiler_params=_params(("arbitrary", "arbitrary")),
        name="attn_prompt",
    )(qi_tab, ki_tab, qkvu, qkvu, qkvu, lam_vec, subln.reshape(1, A_V_DIM))


def _attn_s_kernel(pt_ref, q_ref, kn_ref, vn_ref, lam_ref, sub_ref, *rest, pps, lam_init):
    k_refs = rest[:pps]
    v_refs = rest[pps:2 * pps]
    o_ref, m_ref, l_ref, acc_ref = rest[2 * pps:]
    g = pl.program_id(1)
    nrow = 2 * A_HEADS
    row = lax.broadcasted_iota(jnp.int32, (nrow, QK_WIDTH), 0)
    lane = lax.broadcasted_iota(jnp.int32, (nrow, QK_WIDTH), 1)
    own = (lane // A_QK_DIM) == row
    qmat = jnp.where(own, q_ref[...] * A_SCALE, 0.0).astype(BF16)

    @pl.when(g == 0)
    def _():
        kn = jnp.where(own, kn_ref[...], 0.0).astype(BF16)
        m_ref[...] = jnp.sum(qmat.astype(F32) * kn.astype(F32), axis=1, keepdims=True)
        l_ref[...] = jnp.ones(l_ref.shape, F32)
        acc_ref[...] = jnp.broadcast_to(vn_ref[...].astype(BF16).astype(F32), acc_ref.shape)

    for i in range(pps):
        s = _dot_nt(qmat, k_refs[i][...].astype(BF16))
        m_prev = m_ref[...]
        m_new = jnp.maximum(m_prev, jnp.max(s, axis=1, keepdims=True))
        alpha = jnp.exp(m_prev - m_new)
        e = jnp.exp(s - m_new)
        l_ref[...] = alpha * l_ref[...] + jnp.sum(e, axis=1, keepdims=True)
        acc_ref[...] = alpha * acc_ref[...] + _dot(e.astype(BF16), v_refs[i][...].astype(BF16))
        m_ref[...] = m_new

    @pl.when(g == pl.num_programs(1) - 1)
    def _():
        lam = _lam_value(lam_ref, lam_init)
        r = acc_ref[...] / l_ref[...]
        for h in range(A_HEADS):
            cs = slice(h * A_V_DIM, (h + 1) * A_V_DIM)
            o = r[2 * h:2 * h + 1, cs] - lam * r[2 * h + 1:2 * h + 2, cs]
            o_ref[:, cs] = _rms(o) * sub_ref[...] * (1.0 - lam_init)


def _attn_decode(q, k_new, v_new, cache_k, cache_v, layer, page_table, lam_vec, subln, lam_init):
    b = q.shape[0]
    n_pages = page_table.shape[1]
    pps = DEC_PPS
    page = cache_k.shape[2]
    row_spec = pl.BlockSpec((None, 1, QK_WIDTH), lambda bi, g, pt: (bi, 0, 0))

    def page_spec(i):
        return pl.BlockSpec((None, None, page, QK_WIDTH),
                            lambda bi, g, pt: (layer, pt[bi * n_pages + g * pps + i], 0, 0))

    grid_spec = pltpu.PrefetchScalarGridSpec(
        num_scalar_prefetch=1,
        grid=(b, n_pages // pps),
        in_specs=[row_spec, row_spec, row_spec,
                  pl.BlockSpec((4, A_QK_DIM), lambda bi, g, pt: (0, 0)),
                  pl.BlockSpec((1, A_V_DIM), lambda bi, g, pt: (0, 0))]
                 + [page_spec(i) for i in range(pps)] * 2,
        out_specs=row_spec,
        scratch_shapes=[pltpu.VMEM((2 * A_HEADS, 1), F32), pltpu.VMEM((2 * A_HEADS, 1), F32),
                        pltpu.VMEM((2 * A_HEADS, A_WIDTH), F32)],
    )
    return pl.pallas_call(
        functools.partial(_attn_s_kernel, pps=pps, lam_init=lam_init),
        grid_spec=grid_spec,
        out_shape=jax.ShapeDtypeStruct((b, 1, A_WIDTH), F32),
        compiler_params=_params(("arbitrary", "arbitrary")),
        name="attn_decode",
    )(page_table.reshape(-1), q, k_new, v_new, lam_vec, subln.reshape(1, A_V_DIM),
      *([cache_k] * pps), *([cache_v] * pps))


def _pool_project(pooled, pw_ref, ps_ref, o_ref):
    for g in range(len(POOL_WINDOWS)):
        cs = slice(g * POOL_GROUP_DIM, (g + 1) * POOL_GROUP_DIM)
        o_ref[:, cs] = _dot(pooled[g].astype(BF16), pw_ref[g].astype(BF16)) * ps_ref[:, cs]


def _pool_p_kernel(u_ref, halo_ref, pw_ref, ps_ref, o_ref, s_ref):
    i = pl.program_id(0)
    tm = u_ref.shape[0]
    hr = halo_ref.shape[0]
    s_ref[0:hr] = jnp.where(i > 0, halo_ref[...], 0.0)
    s_ref[hr:] = u_ref[...]
    rows = hr + tm
    d = 1
    while d < POOL_WINDOWS[-1]:
        c0 = sum(POOL_GROUP_DIM for w in POOL_WINDOWS if w <= d)
        s_ref[d:rows, c0:] = s_ref[d:rows, c0:] + s_ref[0:rows - d, c0:]
        d *= 2
    pos = i * tm + lax.broadcasted_iota(jnp.int32, (tm, 1), 0)
    pooled = []
    for g, w in enumerate(POOL_WINDOWS):
        cs = slice(g * POOL_GROUP_DIM, (g + 1) * POOL_GROUP_DIM)
        cnt = jnp.minimum(w, pos + 1).astype(F32)
        pooled.append(s_ref[hr:, cs] / cnt - u_ref[:, cs])
    _pool_project(pooled, pw_ref, ps_ref, o_ref)


def _pool_prompt(qkvu, pool_w, pool_scale):
    m = qkvu.shape[0]
    tm = min(TM, m)
    hr = 2 * SUBLANES
    ucol = (2 * QK_WIDTH + A_WIDTH) // POOL_WIDTH
    return pl.pallas_call(
        _pool_p_kernel,
        grid=(m // tm,),
        in_specs=[
            pl.BlockSpec((tm, POOL_WIDTH), lambda i: (i, ucol)),
            pl.BlockSpec((hr, POOL_WIDTH), lambda i: (jnp.maximum(i * (tm // hr) - 1, 0), ucol)),
            pl.BlockSpec(pool_w.shape, lambda i: (0, 0, 0)),
            pl.BlockSpec((1, POOL_WIDTH), lambda i: (0, 0)),
        ],
        out_specs=pl.BlockSpec((tm, POOL_WIDTH), lambda i: (i, 0)),
        out_shape=jax.ShapeDtypeStruct((m, POOL_WIDTH), F32),
        scratch_shapes=[pltpu.VMEM((hr + tm, POOL_WIDTH), F32)],
        compiler_params=_params(("arbitrary",)),
        name="pool_prompt",
    )(qkvu, qkvu, pool_w, pool_scale.reshape(1, POOL_WIDTH))


def _pool_s_kernel(u_ref, hist_ref, pw_ref, ps_ref, o_ref):
    u = u_ref[...]
    pooled = []
    for g, w in enumerate(POOL_WINDOWS):
        cs = slice(g * POOL_GROUP_DIM, (g + 1) * POOL_GROUP_DIM)
        tot = u[:, cs]
        for k in range(1, w):
            tot = tot + hist_ref[POOL_HIST - k, :, cs]
        pooled.append(tot / float(w) - u[:, cs])
    _pool_project(pooled, pw_ref, ps_ref, o_ref)


def _pool_decode(u, hist_t, pool_w, pool_scale):
    b = u.shape[0]
    return pl.pallas_call(
        _pool_s_kernel,
        grid=(1,),
        in_specs=[
            pl.BlockSpec(u.shape, lambda i: (0, 0)),
            pl.BlockSpec(hist_t.shape, lambda i: (0, 0, 0)),
            pl.BlockSpec(pool_w.shape, lambda i: (0, 0, 0)),
            pl.BlockSpec((1, POOL_WIDTH), lambda i: (0, 0)),
        ],
        out_specs=pl.BlockSpec((b, POOL_WIDTH), lambda i: (0, 0)),
        out_shape=jax.ShapeDtypeStruct((b, POOL_WIDTH), F32),
        compiler_params=_params(("arbitrary",)),
        name="pool_decode",
    )(u, hist_t, pool_w, pool_scale.reshape(1, POOL_WIDTH))


def _softplus(x):
    return jnp.maximum(x, 0.0) + jnp.log1p(jnp.exp(-jnp.abs(x)))


def _conv_silu(raw_ref, tail_ref, ext_ref, w_ref, b_ref):
    q = raw_ref.shape[0]
    t = tail_ref.shape[0]
    ext_ref[0:t] = tail_ref[...]
    ext_ref[t:] = raw_ref[...]
    tail_ref[...] = raw_ref[q - t:q]
    out = b_ref[...]
    for tau in range(SSM_CONV):
        off = t - (SSM_CONV - 1) + tau
        out = out + ext_ref[off:off + q] * w_ref[tau:tau + 1]
    return _silu(out)


def _ssd_p_kernel(x_ref, b_ref, c_ref, z_ref, dt_ref, cwx_ref, cwb_ref, cwc_ref, cbx_ref, cbb_ref, cbc_ref,
                  pr_ref, dsk_ref, nw_ref, y_ref, hout_ref,
                  h_ref, tx_ref, tb_ref, tc_ref, ex_ref, eb_ref, ec_ref):
    c = pl.program_id(1)
    q = x_ref.shape[0]
    gw = x_ref.shape[1]

    @pl.when(c == 0)
    def _():
        h_ref[...] = jnp.zeros(h_ref.shape, F32)
        tx_ref[...] = jnp.zeros(tx_ref.shape, F32)
        tb_ref[...] = jnp.zeros(tb_ref.shape, F32)
        tc_ref[...] = jnp.zeros(tc_ref.shape, F32)

    xs = _conv_silu(x_ref, tx_ref, ex_ref, cwx_ref, cbx_ref)
    bm = _conv_silu(b_ref, tb_ref, eb_ref, cwb_ref, cbb_ref)
    cm = _conv_silu(c_ref, tc_ref, ec_ref, cwc_ref, cbc_ref)

    dt = _softplus(dt_ref[...] + pr_ref[0:1])
    a = -jnp.exp(pr_ref[1:2])
    dta = dt * a

    ri = lax.broadcasted_iota(jnp.int32, (q, q), 0)
    ci = lax.broadcasted_iota(jnp.int32, (q, q), 1)
    causal = ci <= ri
    tri = causal.astype(BF16)
    hi, mid, lo = _split3(dta)
    cum = _dot(tri, hi) + _dot(tri, mid) + _dot(tri, lo)
    cum_t = cum.T

    er = lax.broadcasted_iota(jnp.int32, (LANES, gw), 0)
    el = lax.broadcasted_iota(jnp.int32, (LANES, gw), 1)
    expand = (el // SSM_HEAD_DIM == er).astype(BF16)
    dt_e = _dot_exact_rhs(dt, expand)
    cum_e = _dot_exact_rhs(cum, expand)
    exp_cum_e = jnp.exp(cum_e)
    cum_last_e = cum_e[q - 1:q]
    decay_end_e = jnp.exp(cum_last_e - cum_e)

    xdt = xs * dt_e
    xdt_b = xdt.astype(BF16)
    bm_b = bm.astype(BF16)
    cm_b = cm.astype(BF16)
    cb = _dot_nt(cm_b, bm_b)

    lane = lax.broadcasted_iota(jnp.int32, (q, LANES), 1)
    y_parts = []
    for pp in range(SSM_HPG // 2):
        xp = xdt_b[:, pp * LANES:(pp + 1) * LANES]
        ys = []
        for r in (2 * pp, 2 * pp + 1):
            seg = cum[:, r:r + 1] - cum_t[r:r + 1, :]
            lmat = jnp.exp(jnp.where(causal, seg, NEG_INF))
            ys.append(_dot((cb * lmat).astype(BF16), xp))
        y_parts.append(jnp.where(lane < SSM_HEAD_DIM, ys[0], ys[1]))
    y_diag = jnp.concatenate(y_parts, axis=1)

    h_prev = h_ref[...]
    y_off = _dot(cm_b, h_prev.astype(BF16)) * exp_cum_e
    states = _dot(bm.T.astype(BF16), (xdt * decay_end_e).astype(BF16))
    h_new = h_prev * jnp.exp(cum_last_e) + states
    h_ref[...] = h_new

    y = y_diag + y_off + dsk_ref[...] * xs
    y = y * _silu(z_ref[...])
    y_ref[...] = _rms(y) * nw_ref[...]

    @pl.when(c == pl.num_programs(1) - 1)
    def _():
        hout_ref[...] = h_new


def _ssd_prompt(zx, conv_w, conv_b, params, d_skip_e, norm_w):
    m = zx.shape[0]
    q = min(SSM_CHUNK, m)
    gw = SSM_GROUP_W
    xcol = SSM_INNER // gw
    bcol = 2 * SSM_INNER // SSM_STATE
    ccol = bcol + SSM_GROUPS
    dcol = ccol + SSM_GROUPS
    tail = SUBLANES
    cw2 = conv_w
    cb2 = conv_b.reshape(1, -1)
    in_specs = [
        pl.BlockSpec((q, gw), lambda g, c: (c, xcol + g)),
        pl.BlockSpec((q, SSM_STATE), lambda g, c: (c, bcol + g)),
        pl.BlockSpec((q, SSM_STATE), lambda g, c: (c, ccol + g)),
        pl.BlockSpec((q, gw), lambda g, c: (c, g)),
        pl.BlockSpec((q, LANES), lambda g, c: (c, dcol + g)),
        pl.BlockSpec((SSM_CONV, gw), lambda g, c: (0, g)),
        pl.BlockSpec((SSM_CONV, SSM_STATE), lambda g, c: (0, SSM_INNER // SSM_STATE + g)),
        pl.BlockSpec((SSM_CONV, SSM_STATE), lambda g, c: (0, SSM_INNER // SSM_STATE + SSM_GROUPS + g)),
        pl.BlockSpec((1, gw), lambda g, c: (0, g)),
        pl.BlockSpec((1, SSM_STATE), lambda g, c: (0, SSM_INNER // SSM_STATE + g)),
        pl.BlockSpec((1, SSM_STATE), lambda g, c: (0, SSM_INNER // SSM_STATE + SSM_GROUPS + g)),
        pl.BlockSpec((None, SUBLANES, LANES), lambda g, c: (g, 0, 0)),
        pl.BlockSpec((1, gw), lambda g, c: (0, g)),
        pl.BlockSpec((1, gw), lambda g, c: (0, g)),
    ]
    return pl.pallas_call(
        _ssd_p_kernel,
        grid=(SSM_GROUPS, m // q),
        in_specs=in_specs,
        out_specs=[pl.BlockSpec((q, gw), lambda g, c: (c, g)),
                   pl.BlockSpec((None, SSM_STATE, gw), lambda g, c: (g, 0, 0))],
        out_shape=[jax.ShapeDtypeStruct((m, SSM_INNER), F32),
                   jax.ShapeDtypeStruct((SSM_GROUPS, SSM_STATE, gw), F32)],
        scratch_shapes=[pltpu.VMEM((SSM_STATE, gw), F32),
                        pltpu.VMEM((tail, gw), F32), pltpu.VMEM((tail, SSM_STATE), F32),
                        pltpu.VMEM((tail, SSM_STATE), F32),
                        pltpu.VMEM((tail + q, gw), F32), pltpu.VMEM((tail + q, SSM_STATE), F32),
                        pltpu.VMEM((tail + q, SSM_STATE), F32)],
        compiler_params=_params(("arbitrary", "arbitrary")),
        name="ssd_prompt",
    )(zx, zx, zx, zx, zx, cw2, cw2, cw2, cb2, cb2, cb2, params, d_skip_e.reshape(1, -1), norm_w.reshape(1, -1))


def _ssd_s_pre_kernel(xbc_ref, hist_ref, cw_ref, cb_ref, dt_ref, dtb_ref, alog_ref, xs_ref, bc_ref, xdt_ref, dec_ref):
    out = cb_ref[...] + xbc_ref[...] * cw_ref[SSM_CONV - 1:SSM_CONV]
    for tau in range(SSM_CONV - 1):
        out = out + hist_ref[tau] * cw_ref[tau:tau + 1]
    act = _silu(out)
    xs = act[:, :SSM_INNER]
    dt = _softplus(dt_ref[...] + dtb_ref[...])
    xs_ref[...] = xs
    bc_ref[...] = act[:, SSM_INNER:]
    xdt_ref[...] = xs * dt
    dec_ref[...] = jnp.exp(dt * -jnp.exp(alog_ref[...]))


def _ssd_decode_pre(xbc, hist_t, conv_w, conv_b, dt_e, dtb_e, alog_e):
    b = xbc.shape[0]
    full = lambda a: pl.BlockSpec(a.shape, lambda i: (0,) * a.ndim)
    args = (xbc, hist_t, conv_w, conv_b.reshape(1, -1), dt_e, dtb_e.reshape(1, -1), alog_e.reshape(1, -1))
    shapes = [jax.ShapeDtypeStruct((b, SSM_INNER), F32), jax.ShapeDtypeStruct((b, 2 * SSM_BC_W), F32),
              jax.ShapeDtypeStruct((b, SSM_INNER), F32), jax.ShapeDtypeStruct((b, SSM_INNER), F32)]
    return pl.pallas_call(
        _ssd_s_pre_kernel,
        grid=(1,),
        in_specs=[full(a) for a in args],
        out_specs=[pl.BlockSpec(s.shape, lambda i: (0, 0)) for s in shapes],
        out_shape=shapes,
        compiler_params=_params(("arbitrary",)),
        name="ssd_decode_pre",
    )(*args)


def _ssd_s_state_kernel(h_ref, xdt_ref, dec_ref, b_ref, c_ref, ho_ref, y_ref):
    nblk = xdt_ref.shape[1]
    rows_per_group = SSM_GROUP_W
    lane = lax.broadcasted_iota(jnp.int32, y_ref.shape, 1)
    y = jnp.zeros(y_ref.shape, F32)
    for blk in range(nblk):
        g = (blk * LANES) // rows_per_group
        rs = slice(blk * LANES, (blk + 1) * LANES)
        h_new = (h_ref[rs, :] * dec_ref[:, blk:blk + 1]
                 + xdt_ref[:, blk:blk + 1] * b_ref[:, g * SSM_STATE:(g + 1) * SSM_STATE])
        ho_ref[rs, :] = h_new
        col = jnp.sum(h_new * c_ref[:, g * SSM_STATE:(g + 1) * SSM_STATE], axis=1, keepdims=True)
        y = jnp.where(lane == blk, col, y)
    y_ref[...] = y


def _ssd_decode_state(h0, xdt_t, dec_t, bm, cm):
    b, hp, n = h0.shape
    nblk = hp // LANES
    return pl.pallas_call(
        _ssd_s_state_kernel,
        grid=(b,),
        in_specs=[
            pl.BlockSpec((None, hp, n), lambda i: (i, 0, 0)),
            pl.BlockSpec((None, LANES, nblk), lambda i: (i, 0, 0)),
            pl.BlockSpec((None, LANES, nblk), lambda i: (i, 0, 0)),
            pl.BlockSpec((None, 1, SSM_BC_W), lambda i: (i, 0, 0)),
            pl.BlockSpec((None, 1, SSM_BC_W), lambda i: (i, 0, 0)),
        ],
        out_specs=[pl.BlockSpec((None, hp, n), lambda i: (i, 0, 0)),
                   pl.BlockSpec((None, LANES, nblk), lambda i: (i, 0, 0))],
        out_shape=[jax.ShapeDtypeStruct((b, hp, n), F32), jax.ShapeDtypeStruct((b, LANES, nblk), F32)],
        compiler_params=_params(("arbitrary",)),
        name="ssd_decode_state",
    )(h0, xdt_t, dec_t, bm, cm)


def _ssd_s_post_kernel(y_ref, xs_ref, z_ref, dsk_ref, nw_ref, o_ref):
    y = (y_ref[...] + dsk_ref[...] * xs_ref[...]) * _silu(z_ref[...])
    for g in range(SSM_GROUPS):
        cs = slice(g * SSM_GROUP_W, (g + 1) * SSM_GROUP_W)
        o_ref[:, cs] = _rms(y[:, cs]) * nw_ref[:, cs]


def _ssd_decode_post(y, xs, z, d_skip_e, norm_w):
    args = (y, xs, z, d_skip_e.reshape(1, -1), norm_w.reshape(1, -1))
    return pl.pallas_call(
        _ssd_s_post_kernel,
        grid=(1,),
        in_specs=[pl.BlockSpec(a.shape, lambda i: (0, 0)) for a in args],
        out_specs=pl.BlockSpec(y.shape, lambda i: (0, 0)),
        out_shape=jax.ShapeDtypeStruct(y.shape, F32),
        compiler_params=_params(("arbitrary",)),
        name="ssd_decode_post",
    )(*args)


def _route_rows(x, sc, sh, wr_ref, br_ref, hn_ref, ti_ref, tp_ref):
    rows = x.shape[0]
    hn = _rms(x) * (1.0 + sc) + sh
    hn_ref[0:rows] = hn
    ah, am, al = _split3(hn)
    wh, wm, wl = _split3(wr_ref[...])
    lg = (_dot(ah, wh) + _dot(ah, wm) + _dot(am, wh) + _dot(ah, wl) + _dot(al, wh) + _dot(am, wm)) + br_ref[...]
    lane = lax.broadcasted_iota(jnp.int32, lg.shape, 1)
    lg = jnp.where(lane < N_EXPERTS, lg, NEG_INF)
    vals, idxs = [], []
    for _ in range(TOP_K):
        mx = jnp.max(lg, axis=1, keepdims=True)
        idx = jnp.min(jnp.where(lg == mx, lane, LANES), axis=1, keepdims=True)
        vals.append(mx)
        idxs.append(idx)
        lg = jnp.where(lane == idx, NEG_INF, lg)
    es = [jnp.exp(v - vals[0]) for v in vals]
    tot = es[0] + es[1] + es[2] + es[3]
    ti = jnp.zeros(lg.shape, jnp.int32)
    tp = jnp.zeros(lg.shape, F32)
    for k in range(TOP_K):
        ti = jnp.where(lane == k, idxs[k], ti)
        tp = jnp.where(lane == k, es[k] / tot, tp)
    ti_ref[0:rows] = ti
    tp_ref[0:rows] = tp


def _router_kernel(xp_ref, scp_ref, shp_ref, xs_ref, scs_ref, shs_ref, wr_ref, br_ref, hn_ref, ti_ref, tp_ref):
    last = pl.num_programs(0) - 1

    @pl.when(pl.program_id(0) < last)
    def _():
        _route_rows(xp_ref[...], scp_ref[...], shp_ref[...], wr_ref, br_ref, hn_ref, ti_ref, tp_ref)

    @pl.when(pl.program_id(0) == last)
    def _():
        _route_rows(xs_ref[...], scs_ref[...], shs_ref[...], wr_ref, br_ref, hn_ref, ti_ref, tp_ref)


def _norm_router(xp, sc_p, sh_p, xs, sc_s, sh_s, wr_pad, br_pad):
    mp, d = xp.shape
    nb = xs.shape[0]
    tm = min(TM, mp)
    assert mp % tm == 0 and nb <= tm
    np_ = mp // tm
    t_all = mp + nb
    row = pl.BlockSpec((1, d), lambda i: (0, 0))
    dec = pl.BlockSpec((nb, d), lambda i: (0, 0))
    return pl.pallas_call(
        _router_kernel,
        grid=(np_ + 1,),
        in_specs=[pl.BlockSpec((tm, d), lambda i: (jnp.minimum(i, np_ - 1), 0)), row, row, dec, dec, dec,
                  pl.BlockSpec((d, LANES), lambda i: (0, 0)), pl.BlockSpec((1, LANES), lambda i: (0, 0))],
        out_specs=[pl.BlockSpec((tm, d), lambda i: (i, 0)),
                   pl.BlockSpec((tm, LANES), lambda i: (i, 0)),
                   pl.BlockSpec((tm, LANES), lambda i: (i, 0))],
        out_shape=[jax.ShapeDtypeStruct((t_all, d), F32),
                   jax.ShapeDtypeStruct((t_all, LANES), jnp.int32),
                   jax.ShapeDtypeStruct((t_all, LANES), F32)],
        compiler_params=_params(("arbitrary",)),
        name="norm_router",
    )(xp, sc_p, sh_p, xs, sc_s, sh_s, wr_pad, br_pad)


def _route_plan(top_i, tm, n_tiles):
    t = top_i.shape[0]
    e_flat = top_i.reshape(-1)
    onehot = (e_flat[:, None] == jnp.arange(N_EXPERTS, dtype=jnp.int32)[None, :]).astype(jnp.int32)
    csum = jnp.cumsum(onehot, axis=0)
    rank = jnp.take_along_axis(csum, e_flat[:, None], axis=1)[:, 0] - 1
    cnt = csum[-1]
    tiles_e = (cnt + tm - 1) // tm
    tiles_end = jnp.cumsum(tiles_e)
    tile_start = tiles_end - tiles_e
    dest = (tile_start[e_flat] * tm + rank).astype(jnp.int32)
    total = tiles_end[-1]
    tile_ids = jnp.arange(n_tiles, dtype=jnp.int32)
    tile_valid = (tile_ids < total).astype(jnp.int32)
    tile_expert = jnp.minimum(jnp.searchsorted(tiles_end, tile_ids, side="right"), N_EXPERTS - 1).astype(jnp.int32)
    last_expert = tile_expert[jnp.maximum(total - 1, 0)]
    tile_expert = jnp.where(tile_valid == 1, tile_expert, last_expert)
    local = tile_ids - tile_start[tile_expert]
    tile_rows = jnp.clip(cnt[tile_expert] - local * tm, 0, tm) * tile_valid
    tile_rows = (tile_rows + SUBLANES - 1) // SUBLANES * SUBLANES
    src_tok = jnp.zeros((n_tiles * tm,), jnp.int32).at[dest].set(jnp.arange(t * TOP_K, dtype=jnp.int32) // TOP_K)
    return dest, tile_expert, tile_valid, tile_rows.astype(jnp.int32), src_tok


def _moe_kernel(te_ref, tv_ref, tr_ref, st_ref, hn_hbm, sel_ref, wgu_ref, bgu_ref, wdn_ref, bdn_ref, o_ref,
                xg_ref, xb_ref, sem, *, tm, tn, n_tiles):
    r = pl.program_id(0)
    c = pl.program_id(1)
    slot = r % 2

    def start_gather(tile, s):
        base = tile * tm

        def body(i, carry):
            tok = st_ref[base + i]
            pltpu.make_async_copy(hn_hbm.at[pl.ds(tok, 1)], xg_ref.at[s, pl.ds(i, 1)], sem.at[s]).start()
            return carry

        lax.fori_loop(0, tr_ref[tile], body, 0)

    def wait_gather(tile, s):
        n = pl.multiple_of(tr_ref[tile], SUBLANES)
        pltpu.make_async_copy(hn_hbm.at[pl.ds(0, n)], xg_ref.at[s, pl.ds(0, n)], sem.at[s]).wait()

    @pl.when(c == 0)
    def _():
        @pl.when(r == 0)
        def _():
            xg_ref[...] = jnp.zeros(xg_ref.shape, F32)

            @pl.when(tv_ref[0] == 1)
            def _():
                start_gather(0, 0)

        @pl.when(tv_ref[r] == 1)
        def _():
            wait_gather(r, slot)

        nxt = jnp.minimum(r + 1, n_tiles - 1)

        @pl.when((r + 1 < n_tiles) & (tv_ref[nxt] == 1))
        def _():
            start_gather(nxt, 1 - slot)

        xb_ref[...] = xg_ref[slot].astype(BF16)
        o_ref[...] = jnp.broadcast_to(bdn_ref[...], o_ref.shape)

    @pl.when(tv_ref[r] == 1)
    def _():
        gu = _dot(xb_ref[...], wgu_ref[...].astype(BF16)) + bgu_ref[...]
        up = jnp.concatenate([pltpu.roll(gu[:, k * LANES:(k + 1) * LANES], LANES - 1, 1)
                              for k in range(tn // LANES)], axis=1)
        gate = jnp.minimum(gu, SWIGLU_LIMIT)
        up = jnp.clip(up, -SWIGLU_LIMIT, SWIGLU_LIMIT)
        act = (up + 1.0) * gate * jax.nn.sigmoid(SWIGLU_ALPHA * gate)
        act_c = _dot(act.astype(BF16), sel_ref[...])
        o_ref[...] += _dot(act_c.astype(BF16), wdn_ref[...].astype(BF16))


def _moe_experts(hn_all, plan, w_gu, b_gu, w_dn, b_dn, layer, n_tiles):
    _, tile_expert, tile_valid, tile_rows, src_tok = plan
    d = hn_all.shape[1]
    f2 = w_gu.shape[-1]
    tm, tn = MOE_TM, MOE_TN
    ne = w_gu.shape[1]
    sel = (jnp.arange(tn)[:, None] == 2 * jnp.arange(tn // 2)[None, :]).astype(BF16)
    grid_spec = pltpu.PrefetchScalarGridSpec(
        num_scalar_prefetch=4,
        grid=(n_tiles, f2 // tn),
        in_specs=[
            pl.BlockSpec(memory_space=pl.ANY),
            pl.BlockSpec((tn, tn // 2), lambda r, c, te, tv, tr, st: (0, 0)),
            pl.BlockSpec((None, None, d, tn), lambda r, c, te, tv, tr, st: (layer, te[r], 0, c)),
            pl.BlockSpec((None, None, 1, tn), lambda r, c, te, tv, tr, st: (layer, te[r], 0, c)),
            pl.BlockSpec((None, None, tn // 2, d), lambda r, c, te, tv, tr, st: (layer, te[r], c, 0)),
            pl.BlockSpec((None, None, 1, d), lambda r, c, te, tv, tr, st: (layer, te[r], 0, 0)),
        ],
        out_specs=pl.BlockSpec((tm, d), lambda r, c, te, tv, tr, st: (r, 0)),
        scratch_shapes=[pltpu.VMEM((2, tm, d), F32), pltpu.VMEM((tm, d), BF16), pltpu.SemaphoreType.DMA((2,))],
    )
    return pl.pallas_call(
        functools.partial(_moe_kernel, tm=tm, tn=tn, n_tiles=n_tiles),
        grid_spec=grid_spec,
        out_shape=jax.ShapeDtypeStruct((n_tiles * tm, d), F32),
        compiler_params=_params(("arbitrary", "arbitrary")),
        name="moe_experts",
    )(tile_expert, tile_valid, tile_rows, src_tok, hn_all, sel, w_gu,
      b_gu.reshape(b_gu.shape[0], ne, 1, f2), w_dn, b_dn.reshape(b_dn.shape[0], ne, 1, d))


def _combine_kernel(dest_ref, ys_hbm, tp_ref, x_ref, g_ref, fw_ref, o_ref, yb_ref, sem, *, tc, tok0, final):
    i = pl.program_id(0)
    nsteps = pl.num_programs(0)

    def start_gather(tile, s):
        base = (tok0 + tile * tc) * TOP_K

        def body(j, carry):
            for k in range(TOP_K):
                row = dest_ref[base + j * TOP_K + k]
                pltpu.make_async_copy(ys_hbm.at[pl.ds(row, 1)], yb_ref.at[s, k, pl.ds(j, 1)], sem.at[s]).start()
            return carry

        lax.fori_loop(0, tc, body, 0)

    slot = i % 2

    @pl.when(i == 0)
    def _():
        start_gather(0, 0)

    for k in range(TOP_K):
        pltpu.make_async_copy(ys_hbm.at[pl.ds(0, tc)], yb_ref.at[slot, k], sem.at[slot]).wait()

    @pl.when(i + 1 < nsteps)
    def _():
        start_gather(i + 1, 1 - slot)

    tp = tp_ref[...]
    acc = yb_ref[slot, 0] * tp[:, 0:1]
    for k in range(1, TOP_K):
        acc = acc + yb_ref[slot, k] * tp[:, k:k + 1]
    out = x_ref[...] + g_ref[...] * acc
    if final:
        out = _rms(out) * fw_ref[...]
    o_ref[...] = out


def _moe_combine(dest, ys, top_p, x, gate, final_w, tok0, final):
    m, d = x.shape
    tc = min(256, m)
    mg = gate.shape[0]
    blk0 = tok0 // tc
    grid_spec = pltpu.PrefetchScalarGridSpec(
        num_scalar_prefetch=1,
        grid=(m // tc,),
        in_specs=[
            pl.BlockSpec(memory_space=pl.ANY),
            pl.BlockSpec((tc, LANES), lambda i, dr: (blk0 + i, 0)),
            pl.BlockSpec((tc, d), lambda i, dr: (i, 0)),
            pl.BlockSpec((tc if mg > 1 else 1, d), (lambda i, dr: (i, 0)) if mg > 1 else (lambda i, dr: (0, 0))),
            pl.BlockSpec((1, d), lambda i, dr: (0, 0)),
        ],
        out_specs=pl.BlockSpec((tc, d), lambda i, dr: (i, 0)),
        scratch_shapes=[pltpu.VMEM((2, TOP_K, tc, d), F32), pltpu.SemaphoreType.DMA((2,))],
    )
    return pl.pallas_call(
        functools.partial(_combine_kernel, tc=tc, tok0=tok0, final=final),
        grid_spec=grid_spec,
        out_shape=jax.ShapeDtypeStruct((m, d), F32),
        compiler_params=_params(("arbitrary",)),
        name="moe_combine",
    )(dest, ys, top_p, x, gate, final_w.reshape(1, d))


def _rope_tables(pos):
    half = A_QK_DIM // 2
    inv = ROPE_THETA ** (-jnp.arange(half, dtype=F32) * 2.0 / A_QK_DIM)
    ang = pos.astype(F32)[:, None] * inv[None, :]
    cos, sin = jnp.cos(ang), jnp.sin(ang)
    cos_t = jnp.tile(cos, (1, LANES // half))
    sin_t = jnp.tile(jnp.concatenate([-sin, sin], axis=1), (1, LANES // A_QK_DIM))
    return cos_t, sin_t


def kernel(x_prompt, x_sample, cache_k, cache_v, state_pool, state_conv, state_ssm, page_table, c_prompt, c_sample,
           ada_w, ada_b, ab_w_in, ab_lambda, ab_subln, ab_pool_w, ab_pool_scale, ab_w_out,
           ssd_w_in, ssd_conv_w, ssd_conv_b, ssd_dt_bias, ssd_a_log, ssd_d, ssd_norm, ssd_w_out,
           router_w, router_b, moe_w_gu, moe_b_gu, moe_w_dn, moe_b_dn, final_norm):
    d = D_MODEL
    seq = x_prompt.shape[1]
    nb = x_sample.shape[0]
    n_pages = page_table.shape[1]
    past_len = n_pages * cache_k.shape[2]
    t_all = seq + nb
    n_tiles = -(-(t_all * TOP_K) // MOE_TM) + N_EXPERTS

    c_rows = SUBLANES * (-(-(1 + nb) // SUBLANES))
    c_all = jnp.concatenate([c_prompt, c_sample, jnp.zeros((c_rows - 1 - nb, d), F32)], axis=0)
    mod = _ada_mod(c_all, ada_w, ada_b).reshape(DEPTH, c_rows, N_MOD, d)

    def mods(layer, m):
        return mod[layer, 0:1, m], mod[layer, 1:1 + nb, m]

    xp = x_prompt.reshape(seq, d)
    xs_ = x_sample.reshape(nb, d)
    rope_p = _rope_tables(jnp.arange(seq))
    rope_s = _rope_tables(jnp.full((nb,), past_len))
    cache_k4 = cache_k.reshape(cache_k.shape[0], cache_k.shape[1], cache_k.shape[2], QK_WIDTH)
    cache_v4 = cache_v.reshape(cache_v.shape[0], cache_v.shape[1], cache_v.shape[2], A_WIDTH)

    def moe_block(layer, xp, xs_, final):
        sh_p, sh_s = mods(layer, 3)
        sc_p, sc_s = mods(layer, 4)
        g_p, g_s = mods(layer, 5)
        wr = jnp.pad(router_w[layer], ((0, 0), (0, LANES - N_EXPERTS)))
        br = jnp.pad(router_b[layer], (0, LANES - N_EXPERTS)).reshape(1, LANES)
        hn_all, ti_all, tp_all = _norm_router(xp, sc_p, sh_p, xs_, sc_s, sh_s, wr, br)
        plan = _route_plan(ti_all[:, :TOP_K], MOE_TM, n_tiles)
        ys = _moe_experts(hn_all, plan, moe_w_gu, moe_b_gu, moe_w_dn, moe_b_dn, layer, n_tiles)
        xp = _moe_combine(plan[0], ys, tp_all, xp, g_p, final_norm, 0, final)
        xs_ = _moe_combine(plan[0], ys, tp_all, xs_, g_s, final_norm, seq, final)
        return xp, xs_

    sh_p, sh_s = mods(0, 0)
    sc_p, sc_s = mods(0, 1)
    g_p, g_s = mods(0, 2)
    lam0 = _lambda_init(0)
    w_in0 = ab_w_in[0].astype(BF16)
    w_out0 = ab_w_out[0].astype(BF16)
    ucol = 2 * QK_WIDTH + A_WIDTH

    qkvu_p = _norm_matmul(xp, sc_p, sh_p, w_in0, 1024, rope_p, 2 * QK_WIDTH)
    o_p = _attn_prompt(qkvu_p, ab_lambda[0], ab_subln[0], lam0)
    pg_p = _pool_prompt(qkvu_p, ab_pool_w[0], ab_pool_scale[0])
    k_prompt = qkvu_p[:, QK_WIDTH:2 * QK_WIDTH].reshape(1, 1, seq, A_HEADS, 2, A_QK_DIM)
    v_prompt = qkvu_p[:, 2 * QK_WIDTH:ucol].reshape(1, 1, seq, A_HEADS, A_V_DIM)
    pool_prompt = qkvu_p[seq - POOL_HIST:, ucol:].reshape(1, 1, POOL_HIST, POOL_WIDTH)
    xp = _matmul_resid([o_p, pg_p], w_out0, xp, g_p)

    qkvu_s = _norm_matmul(xs_, sc_s, sh_s, w_in0, 1024, rope_s, 2 * QK_WIDTH)
    q_s = qkvu_s[:, :QK_WIDTH].reshape(nb, 1, QK_WIDTH)
    k_s = qkvu_s[:, QK_WIDTH:2 * QK_WIDTH].reshape(nb, 1, QK_WIDTH)
    v_s = qkvu_s[:, 2 * QK_WIDTH:ucol].reshape(nb, 1, A_WIDTH)
    u_s = qkvu_s[:, ucol:]
    o_s = _attn_decode(q_s, k_s, v_s, cache_k4, cache_v4, 0, page_table, ab_lambda[0], ab_subln[0], lam0)
    pg_s = _pool_decode(u_s, jnp.swapaxes(state_pool[0], 0, 1), ab_pool_w[0], ab_pool_scale[0])
    k_sample = k_s.reshape(1, nb, 1, A_HEADS, 2, A_QK_DIM)
    v_sample = v_s.reshape(1, nb, 1, A_HEADS, A_V_DIM)
    pool_sample = jnp.concatenate([state_pool[0][:, 1:], u_s[:, None, :]], axis=1)[None]
    xs_ = _matmul_resid([o_s.reshape(nb, A_WIDTH), pg_s], w_out0, xs_, g_s)

    xp, xs_ = moe_block(0, xp, xs_, False)

    sh_p, sh_s = mods(1, 0)
    sc_p, sc_s = mods(1, 1)
    g_p, g_s = mods(1, 2)
    zx_w = SSM_INNER + SSM_CONV_DIM
    w_dt = ssd_w_in[0][:, zx_w:].reshape(d, SSM_GROUPS, SSM_HPG)
    w_dt = jnp.pad(w_dt, ((0, 0), (0, 0), (0, LANES - SSM_HPG))).reshape(d, SSM_GROUPS * LANES)
    w_in1 = jnp.concatenate([ssd_w_in[0][:, :zx_w], w_dt], axis=1).astype(BF16)
    w_out1 = ssd_w_out[0].astype(BF16)
    tn1 = w_in1.shape[1] // SSM_GROUPS

    def per_group(v):
        return jnp.pad(v.reshape(SSM_GROUPS, SSM_HPG), ((0, 0), (0, LANES - SSM_HPG)))

    zeros_g = jnp.zeros((SSM_GROUPS, LANES), F32)
    params = jnp.stack([per_group(ssd_dt_bias[0]), per_group(ssd_a_log[0])] + [zeros_g] * (SUBLANES - 2), axis=1)
    dsk_e = jnp.repeat(ssd_d[0], SSM_HEAD_DIM)
    dtb_e = jnp.repeat(ssd_dt_bias[0], SSM_HEAD_DIM)
    alog_e = jnp.repeat(ssd_a_log[0], SSM_HEAD_DIM)

    zx_p = _norm_matmul(xp, sc_p, sh_p, w_in1, tn1)
    y_p, h_p = _ssd_prompt(zx_p, ssd_conv_w[0], ssd_conv_b[0], params, dsk_e, ssd_norm[0])
    conv_prompt = zx_p[seq - (SSM_CONV - 1):, SSM_INNER:zx_w].reshape(1, 1, SSM_CONV - 1, SSM_CONV_DIM)
    ssm_prompt = jnp.transpose(h_p.reshape(SSM_GROUPS, SSM_STATE, SSM_HPG, SSM_HEAD_DIM), (0, 2, 3, 1))
    ssm_prompt = ssm_prompt.reshape(1, 1, SSM_HEADS, SSM_HEAD_DIM, SSM_STATE)
    xp = _matmul_resid([y_p], w_out1, xp, g_p)

    zx_s = _norm_matmul(xs_, sc_s, sh_s, w_in1, tn1)
    z_s = zx_s[:, :SSM_INNER]
    xbc_s = zx_s[:, SSM_INNER:zx_w]
    dt_s = zx_s[:, zx_w:].reshape(nb, SSM_GROUPS, LANES)[:, :, :SSM_HPG].reshape(nb, SSM_HEADS)
    dt_e = jnp.repeat(dt_s, SSM_HEAD_DIM, axis=1)
    xs_act, bc_s, xdt_s, dec_s = _ssd_decode_pre(xbc_s, jnp.swapaxes(state_conv[0], 0, 1), ssd_conv_w[0],
                                                 ssd_conv_b[0], dt_e, dtb_e, alog_e)
    nblk = SSM_INNER // LANES

    def col_major(v):
        return jnp.swapaxes(v.reshape(nb, nblk, LANES), 1, 2)

    h_s, y_t = _ssd_decode_state(state_ssm[0].reshape(nb, SSM_INNER, SSM_STATE), col_major(xdt_s), col_major(dec_s),
                                 bc_s[:, :SSM_BC_W].reshape(nb, 1, SSM_BC_W), bc_s[:, SSM_BC_W:].reshape(nb, 1, SSM_BC_W))
    y_s = jnp.swapaxes(y_t, 1, 2).reshape(nb, SSM_INNER)
    yn_s = _ssd_decode_post(y_s, xs_act, z_s, dsk_e, ssd_norm[0])
    conv_sample = jnp.concatenate([state_conv[0][:, 1:], xbc_s[:, None, :]], axis=1)[None]
    ssm_sample = h_s.reshape(1, nb, SSM_HEADS, SSM_HEAD_DIM, SSM_STATE)
    xs_ = _matmul_resid([yn_s], w_out1, xs_, g_s)

    xp, xs_ = moe_block(1, xp, xs_, True)

    return (xp.reshape(1, seq, d), xs_.reshape(nb, 1, d), k_prompt, v_prompt, pool_prompt, conv_prompt, ssm_prompt,
            k_sample, v_sample, pool_sample, conv_sample, ssm_sample)
```

```python
import functools
import math

import jax
import jax.numpy as jnp
from jax import lax
from jax.experimental import pallas as pl
from jax.experimental.pallas import tpu as pltpu

F32 = jnp.float32
BF16 = jnp.bfloat16
NEG_INF = float("-inf")
LOG2_E = math.log2(math.e)
HI_HALF_MASK = -65536

D_MODEL = 2048
DEPTH = 2
PAGE_SIZE = 128
NORM_EPS = 1e-5
N_MOD = 6
A_HEADS = 8
A_QK_DIM = 64
A_V_DIM = 128
QK_WIDTH = A_HEADS * 2 * A_QK_DIM
A_WIDTH = A_HEADS * A_V_DIM
A_SCALE = A_QK_DIM ** -0.5
ROPE_THETA = 10000.0
POOL_WINDOWS = (2, 4, 8, 16)
POOL_WIDTH = D_MODEL // 2
POOL_GROUP_DIM = POOL_WIDTH // 4
POOL_HIST = 15
SSM_INNER = 2 * D_MODEL
SSM_HEAD_DIM = 64
SSM_HEADS = SSM_INNER // SSM_HEAD_DIM
SSM_GROUPS = 8
SSM_HPG = SSM_HEADS // SSM_GROUPS
SSM_STATE = 128
SSM_CONV = 4
SSM_CHUNK = 128
SSM_GROUP_W = SSM_INNER // SSM_GROUPS
SSM_BC_W = SSM_GROUPS * SSM_STATE
SSM_CONV_DIM = SSM_INNER + 2 * SSM_BC_W
N_EXPERTS = 32
TOP_K = 4
SWIGLU_ALPHA = 1.702
SWIGLU_LIMIT = 7.0

LANES = 128
SUBLANES = 8
VMEM_LIMIT = 56 * 1024 * 1024

TM = 512
MOE_TN = 1024
MOE_SEL_W = 512
ATT_TQ = 1024
ATT_TK = 512
DEC_PPS = 8


def _params(sem):
    return pltpu.CompilerParams(dimension_semantics=sem, vmem_limit_bytes=VMEM_LIMIT)


def _lambda_init(layer):
    return 0.8 - 0.6 * math.exp(-0.3 * layer)


def _split3(x):
    hi = x.astype(BF16)
    r1 = x - hi.astype(F32)
    mid = r1.astype(BF16)
    lo = (r1 - mid.astype(F32)).astype(BF16)
    return hi, mid, lo


def _dot(a, b):
    return jnp.dot(a, b, preferred_element_type=F32)


def _dot_nt(a, b):
    return lax.dot_general(a, b, (((1,), (1,)), ((), ())), preferred_element_type=F32)


def _dot_exact_rhs(x, sel):
    hi, mid, lo = _split3(x)
    return _dot(hi, sel) + _dot(mid, sel) + _dot(lo, sel)


def _silu(x):
    return x * jax.nn.sigmoid(x)


def _rms(x):
    return x * lax.rsqrt(jnp.mean(x * x, axis=-1, keepdims=True) + NORM_EPS)


def _ada_kernel(c_ref, w_ref, b_ref, o_ref):
    a = _silu(c_ref[...]).astype(BF16)
    o_ref[...] = _dot(a, w_ref[...].astype(BF16)) + b_ref[...]


def _ada_mod(c_all, ada_w, ada_b, tn=1024):
    m, d = c_all.shape
    nl, _, n = ada_w.shape
    return pl.pallas_call(
        _ada_kernel,
        grid=(nl, n // tn),
        in_specs=[
            pl.BlockSpec((m, d), lambda l, j: (0, 0)),
            pl.BlockSpec((None, d, tn), lambda l, j: (l, 0, j)),
            pl.BlockSpec((None, 1, tn), lambda l, j: (l, 0, j)),
        ],
        out_specs=pl.BlockSpec((None, m, tn), lambda l, j: (l, 0, j)),
        out_shape=jax.ShapeDtypeStruct((nl, m, n), F32),
        compiler_params=_params(("arbitrary", "arbitrary")),
        name="ada_mod",
    )(c_all, ada_w, ada_b.reshape(nl, 1, n))


def _norm_mm_kernel(*refs, rope_tiles, tn):
    if rope_tiles:
        x_ref, sc_ref, sh_ref, w_ref, cos_ref, sin_ref, o_ref, hn_ref = refs
    else:
        x_ref, sc_ref, sh_ref, w_ref, o_ref, hn_ref = refs
    j = pl.program_id(1)

    @pl.when(j == 0)
    def _():
        hn_ref[...] = (_rms(x_ref[...]) * (1.0 + sc_ref[...]) + sh_ref[...]).astype(BF16)

    y = _dot(hn_ref[...], w_ref[...])
    if not rope_tiles:
        o_ref[...] = y
        return

    @pl.when(j >= rope_tiles)
    def _():
        o_ref[...] = y

    @pl.when(j < rope_tiles)
    def _():
        cos = cos_ref[...]
        sin = sin_ref[...]
        lane = lax.broadcasted_iota(jnp.int32, cos.shape, 1)
        first = (lane % A_QK_DIM) < (A_QK_DIM // 2)
        for c in range(tn // LANES):
            yc = y[:, c * LANES:(c + 1) * LANES]
            partner = jnp.where(first, pltpu.roll(yc, LANES - A_QK_DIM // 2, 1),
                                pltpu.roll(yc, A_QK_DIM // 2, 1))
            o_ref[:, c * LANES:(c + 1) * LANES] = yc * cos + partner * sin


def _norm_matmul(x, scale, shift, w, tn, rope=None, rope_cols=0):
    m, d = x.shape
    n = w.shape[1]
    tm = min(TM, m)
    ms = scale.shape[0]
    mod_spec = pl.BlockSpec((tm if ms > 1 else 1, d), (lambda i, j: (i, 0)) if ms > 1 else (lambda i, j: (0, 0)))
    in_specs = [pl.BlockSpec((tm, d), lambda i, j: (i, 0)), mod_spec, mod_spec,
                pl.BlockSpec((d, tn), lambda i, j: (0, j))]
    args = [x, scale, shift, w]
    if rope is not None:
        in_specs += [pl.BlockSpec((tm, LANES), lambda i, j: (i, 0))] * 2
        args += list(rope)
    return pl.pallas_call(
        functools.partial(_norm_mm_kernel, rope_tiles=rope_cols // tn, tn=tn),
        grid=(m // tm, n // tn),
        in_specs=in_specs,
        out_specs=pl.BlockSpec((tm, tn), lambda i, j: (i, j)),
        out_shape=jax.ShapeDtypeStruct((m, n), F32),
        scratch_shapes=[pltpu.VMEM((tm, d), BF16)],
        compiler_params=_params(("arbitrary", "arbitrary")),
        name="norm_matmul",
    )(*args)


def _mm_resid_kernel(*refs, n_a):
    a_refs = refs[:n_a]
    w_ref, x_ref, g_ref, o_ref, ab_ref = refs[n_a:]

    @pl.when(pl.program_id(1) == 0)
    def _():
        off = 0
        for a_ref in a_refs:
            ka = a_ref.shape[1]
            ab_ref[:, off:off + ka] = a_ref[...].astype(BF16)
            off += ka

    o_ref[...] = x_ref[...] + g_ref[...] * _dot(ab_ref[...], w_ref[...])


def _matmul_resid(a_list, w, x, gate, tn=1024):
    m, n = x.shape
    k = w.shape[0]
    tm = min(TM, m)
    mg = gate.shape[0]
    in_specs = [pl.BlockSpec((tm, a.shape[1]), lambda i, j: (i, 0)) for a in a_list]
    in_specs += [
        pl.BlockSpec((k, tn), lambda i, j: (0, j)),
        pl.BlockSpec((tm, tn), lambda i, j: (i, j)),
        pl.BlockSpec((tm if mg > 1 else 1, tn), (lambda i, j: (i, j)) if mg > 1 else (lambda i, j: (0, j))),
    ]
    return pl.pallas_call(
        functools.partial(_mm_resid_kernel, n_a=len(a_list)),
        grid=(m // tm, n // tn),
        in_specs=in_specs,
        out_specs=pl.BlockSpec((tm, tn), lambda i, j: (i, j)),
        out_shape=jax.ShapeDtypeStruct((m, n), F32),
        scratch_shapes=[pltpu.VMEM((tm, k), BF16)],
        compiler_params=_params(("arbitrary", "arbitrary")),
        name="matmul_resid",
    )(*a_list, w, x, gate)


def _lam_value(lam_ref, lam_init):
    lv = lam_ref[...]
    s01 = jnp.sum(lv[0:1] * lv[1:2], axis=1, keepdims=True)
    s23 = jnp.sum(lv[2:3] * lv[3:4], axis=1, keepdims=True)
    return jnp.exp(s01) - jnp.exp(s23) + lam_init


def _attn_p_kernel(qi_ref, ki_ref, q_ref, k_ref, v_ref, lam_ref, sub_ref, o_ref, m_ref, l_ref, acc_ref, *, lam_init):
    p = pl.program_id(1)
    qi = qi_ref[p]
    ki = ki_ref[p]
    tq = q_ref.shape[0]
    tk = k_ref.shape[0]
    ratio = tq // tk
    shift = ki * tk - qi * tq

    @pl.when(ki == 0)
    def _():
        m_ref[...] = jnp.full(m_ref.shape, NEG_INF, F32)
        l_ref[...] = jnp.zeros(l_ref.shape, F32)
        acc_ref[...] = jnp.zeros(acc_ref.shape, F32)

    def step(masked):
        q = q_ref[...] * (A_SCALE * LOG2_E)
        lane = lax.broadcasted_iota(jnp.int32, q.shape, 1)
        kb = k_ref[...].astype(BF16)
        vb = v_ref[...].astype(BF16)
        if masked:
            row = lax.broadcasted_iota(jnp.int32, (tq, LANES), 0)
            col = lax.broadcasted_iota(jnp.int32, (tq, LANES), 1) + shift
        for c in range(2):
            qc = jnp.where((lane < A_QK_DIM) == (c == 0), q, 0.0).astype(BF16)
            s = _dot_nt(qc, kb)
            chunks = [s[:, j * LANES:(j + 1) * LANES] for j in range(tk // LANES)]
            if masked:
                chunks = [jnp.where(col + j * LANES <= row, ch, NEG_INF) for j, ch in enumerate(chunks)]
            mx = chunks[0]
            for ch in chunks[1:]:
                mx = jnp.maximum(mx, ch)
            m_prev = m_ref[c]
            m_new = jnp.maximum(m_prev, jnp.max(mx, axis=1, keepdims=True))
            alpha = jnp.exp2(m_prev - m_new)
            es = [jnp.exp2(ch - m_new) for ch in chunks]
            tot = es[0]
            for e in es[1:]:
                tot = tot + e
            l_ref[c] = alpha * l_ref[c] + tot
            acc_ref[c] = alpha * acc_ref[c] + _dot(jnp.concatenate(es, axis=1).astype(BF16), vb)
            m_ref[c] = m_new

    @pl.when(ki < ratio * qi)
    def _():
        step(False)

    @pl.when(ki >= ratio * qi)
    def _():
        step(True)

    @pl.when(ki == ratio * (qi + 1) - 1)
    def _():
        lam = _lam_value(lam_ref, lam_init)
        l0 = jnp.sum(l_ref[0], axis=1, keepdims=True)
        l1 = jnp.sum(l_ref[1], axis=1, keepdims=True)
        o = acc_ref[0] / l0 - lam * (acc_ref[1] / l1)
        o_ref[...] = _rms(o) * sub_ref[...] * (1.0 - lam_init)


def _attn_prompt(qkvu, lam_vec, subln, lam_init):
    m = qkvu.shape[0]
    tk = min(ATT_TK, m)
    tq = min(ATT_TQ, m)
    assert m % tq == 0 and tq % tk == 0
    ratio = tq // tk
    pairs = [(qi, ki) for qi in range(m // tq) for ki in range(ratio * (qi + 1))]
    qi_tab = jnp.asarray([p[0] for p in pairs], jnp.int32)
    ki_tab = jnp.asarray([p[1] for p in pairs], jnp.int32)
    kcol = QK_WIDTH // LANES
    vcol = 2 * QK_WIDTH // LANES
    grid_spec = pltpu.PrefetchScalarGridSpec(
        num_scalar_prefetch=2,
        grid=(A_HEADS, len(pairs)),
        in_specs=[
            pl.BlockSpec((tq, LANES), lambda h, p, qt, kt: (qt[p], h)),
            pl.BlockSpec((tk, LANES), lambda h, p, qt, kt: (kt[p], kcol + h)),
            pl.BlockSpec((tk, LANES), lambda h, p, qt, kt: (kt[p], vcol + h)),
            pl.BlockSpec((4, A_QK_DIM), lambda h, p, qt, kt: (0, 0)),
            pl.BlockSpec((1, A_V_DIM), lambda h, p, qt, kt: (0, 0)),
        ],
        out_specs=pl.BlockSpec((tq, LANES), lambda h, p, qt, kt: (qt[p], h)),
        scratch_shapes=[pltpu.VMEM((2, tq, LANES), F32)] * 3,
    )
    return pl.pallas_call(
        functools.partial(_attn_p_kernel, lam_init=lam_init),
        grid_spec=grid_spec,
        out_shape=jax.ShapeDtypeStruct((m, A_WIDTH), F32),
        compiler_params=_params(("arbitrary", "arbitrary")),
        name="attn_prompt",
    )(qi_tab, ki_tab, qkvu, qkvu, qkvu, lam_vec, subln.reshape(1, A_V_DIM))


def _attn_s_kernel(pt_ref, q_ref, kn_ref, vn_ref, lam_ref, sub_ref, rep_ref, *rest, pps, lam_init):
    kt_refs = rest[:pps]
    v_refs = rest[pps:2 * pps]
    o_ref, m_ref, l_ref, acc_ref = rest[2 * pps:]
    g = pl.program_id(1)
    nrow = 2 * A_HEADS
    row = lax.broadcasted_iota(jnp.int32, (nrow, QK_WIDTH), 0)
    lane = lax.broadcasted_iota(jnp.int32, (nrow, QK_WIDTH), 1)
    own = (lane // A_QK_DIM) == row
    qmat = jnp.where(own, q_ref[...] * A_SCALE, 0.0).astype(BF16)
    page = kt_refs[0].shape[1]

    @pl.when(g == 0)
    def _():
        kn = jnp.where(own, kn_ref[...], 0.0).astype(BF16)
        m_ref[...] = jnp.sum(qmat.astype(F32) * kn.astype(F32), axis=1, keepdims=True)
        l_ref[...] = jnp.ones(l_ref.shape, F32)
        vn = vn_ref[...].astype(BF16).astype(F32)
        for h in range(A_HEADS):
            acc_ref[2 * h:2 * h + 2, :] = jnp.broadcast_to(vn[:, h * A_V_DIM:(h + 1) * A_V_DIM], (2, A_V_DIM))

    s = jnp.concatenate([_dot(qmat, kt_refs[i][...].astype(BF16)) for i in range(pps)], axis=1)
    m_prev = m_ref[...]
    m_new = jnp.maximum(m_prev, jnp.max(s, axis=1, keepdims=True))
    alpha = jnp.exp(m_prev - m_new)
    e = jnp.exp(s - m_new)
    l_ref[...] = alpha * l_ref[...] + jnp.sum(e, axis=1, keepdims=True)
    m_ref[...] = m_new
    eb = jnp.concatenate([e[:, i * page:(i + 1) * page] for i in range(pps)], axis=0).astype(BF16)
    spread = _dot(eb, rep_ref[...])
    vrow = lax.broadcasted_iota(jnp.int32, (nrow, page * A_HEADS), 0)
    vlane = lax.broadcasted_iota(jnp.int32, (nrow, page * A_HEADS), 1)
    mine = (vlane % A_HEADS) == (vrow // 2)
    w = jnp.concatenate([jnp.where(mine, spread[i * nrow:(i + 1) * nrow], 0.0).astype(BF16) for i in range(pps)],
                        axis=1)
    v_all = jnp.concatenate([v_refs[i][...].astype(BF16) for i in range(pps)], axis=0)
    acc_ref[...] = alpha * acc_ref[...] + _dot(w, v_all)

    @pl.when(g == pl.num_programs(1) - 1)
    def _():
        lam = _lam_value(lam_ref, lam_init)
        r = acc_ref[...] / l_ref[...]
        for h in range(A_HEADS):
            o = r[2 * h:2 * h + 1] - lam * r[2 * h + 1:2 * h + 2]
            o_ref[:, h * A_V_DIM:(h + 1) * A_V_DIM] = _rms(o) * sub_ref[...] * (1.0 - lam_init)


def _attn_decode(q, k_new, v_new, cache_k, cache_v, layer, page_table, lam_vec, subln, lam_init):
    b = q.shape[0]
    n_pages = page_table.shape[1]
    pps = math.gcd(DEC_PPS, n_pages)
    page = cache_k.shape[3]
    row_spec = pl.BlockSpec((None, 1, QK_WIDTH), lambda bi, g, pt: (bi, 0, 0))
    rep = (jnp.arange(page * A_HEADS)[None, :] // A_HEADS == jnp.arange(page)[:, None]).astype(BF16)

    def page_spec(i, rows, cols):
        return pl.BlockSpec((None, None, rows, cols),
                            lambda bi, g, pt: (layer, pt[bi * n_pages + g * pps + i], 0, 0))

    grid_spec = pltpu.PrefetchScalarGridSpec(
        num_scalar_prefetch=1,
        grid=(b, n_pages // pps),
        in_specs=[row_spec, row_spec, row_spec,
                  pl.BlockSpec((4, A_QK_DIM), lambda bi, g, pt: (0, 0)),
                  pl.BlockSpec((1, A_V_DIM), lambda bi, g, pt: (0, 0)),
                  pl.BlockSpec((page, page * A_HEADS), lambda bi, g, pt: (0, 0))]
                 + [page_spec(i, QK_WIDTH, page) for i in range(pps)]
                 + [page_spec(i, page * A_HEADS, A_V_DIM) for i in range(pps)],
        out_specs=row_spec,
        scratch_shapes=[pltpu.VMEM((2 * A_HEADS, 1), F32), pltpu.VMEM((2 * A_HEADS, 1), F32),
                        pltpu.VMEM((2 * A_HEADS, A_V_DIM), F32)],
    )
    return pl.pallas_call(
        functools.partial(_attn_s_kernel, pps=pps, lam_init=lam_init),
        grid_spec=grid_spec,
        out_shape=jax.ShapeDtypeStruct((b, 1, A_WIDTH), F32),
        compiler_params=_params(("arbitrary", "arbitrary")),
        name="attn_decode",
    )(page_table.reshape(-1), q, k_new, v_new, lam_vec, subln.reshape(1, A_V_DIM), rep,
      *([cache_k] * pps), *([cache_v] * pps))


def _pool_project(pooled, pw_ref, ps_ref, o_ref):
    for g in range(len(POOL_WINDOWS)):
        cs = slice(g * POOL_GROUP_DIM, (g + 1) * POOL_GROUP_DIM)
        o_ref[:, cs] = _dot(pooled[g].astype(BF16), pw_ref[g].astype(BF16)) * ps_ref[:, cs]


def _pool_p_kernel(u_ref, halo_ref, pw_ref, ps_ref, o_ref, s_ref):
    i = pl.program_id(0)
    tm = u_ref.shape[0]
    hr = halo_ref.shape[0]
    s_ref[0:hr] = jnp.where(i > 0, halo_ref[...], 0.0)
    s_ref[hr:] = u_ref[...]
    rows = hr + tm
    d = 1
    while d < POOL_WINDOWS[-1]:
        c0 = sum(POOL_GROUP_DIM for w in POOL_WINDOWS if w <= d)
        s_ref[d:rows, c0:] = s_ref[d:rows, c0:] + s_ref[0:rows - d, c0:]
        d *= 2
    pos = i * tm + lax.broadcasted_iota(jnp.int32, (tm, 1), 0)
    pooled = []
    for g, w in enumerate(POOL_WINDOWS):
        cs = slice(g * POOL_GROUP_DIM, (g + 1) * POOL_GROUP_DIM)
        cnt = jnp.minimum(w, pos + 1).astype(F32)
        pooled.append(s_ref[hr:, cs] / cnt - u_ref[:, cs])
    _pool_project(pooled, pw_ref, ps_ref, o_ref)


def _pool_prompt(qkvu, pool_w, pool_scale):
    m = qkvu.shape[0]
    tm = min(TM, m)
    hr = 2 * SUBLANES
    ucol = (2 * QK_WIDTH + A_WIDTH) // POOL_WIDTH
    return pl.pallas_call(
        _pool_p_kernel,
        grid=(m // tm,),
        in_specs=[
            pl.BlockSpec((tm, POOL_WIDTH), lambda i: (i, ucol)),
            pl.BlockSpec((hr, POOL_WIDTH), lambda i: (jnp.maximum(i * (tm // hr) - 1, 0), ucol)),
            pl.BlockSpec(pool_w.shape, lambda i: (0, 0, 0)),
            pl.BlockSpec((1, POOL_WIDTH), lambda i: (0, 0)),
        ],
        out_specs=pl.BlockSpec((tm, POOL_WIDTH), lambda i: (i, 0)),
        out_shape=jax.ShapeDtypeStruct((m, POOL_WIDTH), F32),
        scratch_shapes=[pltpu.VMEM((hr + tm, POOL_WIDTH), F32)],
        compiler_params=_params(("arbitrary",)),
        name="pool_prompt",
    )(qkvu, qkvu, pool_w, pool_scale.reshape(1, POOL_WIDTH))


def _pool_s_kernel(u_ref, hist_ref, pw_ref, ps_ref, o_ref):
    u = u_ref[...]
    pooled = []
    for g, w in enumerate(POOL_WINDOWS):
        cs = slice(g * POOL_GROUP_DIM, (g + 1) * POOL_GROUP_DIM)
        tot = u[:, cs]
        for k in range(1, w):
            tot = tot + hist_ref[POOL_HIST - k, :, cs]
        pooled.append(tot / float(w) - u[:, cs])
    _pool_project(pooled, pw_ref, ps_ref, o_ref)


def _pool_decode(u, hist_t, pool_w, pool_scale):
    b = u.shape[0]
    return pl.pallas_call(
        _pool_s_kernel,
        grid=(1,),
        in_specs=[
            pl.BlockSpec(u.shape, lambda i: (0, 0)),
            pl.BlockSpec(hist_t.shape, lambda i: (0, 0, 0)),
            pl.BlockSpec(pool_w.shape, lambda i: (0, 0, 0)),
            pl.BlockSpec((1, POOL_WIDTH), lambda i: (0, 0)),
        ],
        out_specs=pl.BlockSpec((b, POOL_WIDTH), lambda i: (0, 0)),
        out_shape=jax.ShapeDtypeStruct((b, POOL_WIDTH), F32),
        compiler_params=_params(("arbitrary",)),
        name="pool_decode",
    )(u, hist_t, pool_w, pool_scale.reshape(1, POOL_WIDTH))


def _softplus(x):
    return jnp.maximum(x, 0.0) + jnp.log1p(jnp.exp(-jnp.abs(x)))


def _conv_silu(raw_ref, tail_ref, ext_ref, w_ref, b_ref):
    q = raw_ref.shape[0]
    t = tail_ref.shape[0]
    ext_ref[0:t] = tail_ref[...]
    ext_ref[t:] = raw_ref[...]
    tail_ref[...] = raw_ref[q - t:q]
    out = b_ref[...]
    for tau in range(SSM_CONV):
        off = t - (SSM_CONV - 1) + tau
        out = out + ext_ref[off:off + q] * w_ref[tau:tau + 1]
    return _silu(out)


def _ssd_p_kernel(x_ref, b_ref, c_ref, z_ref, dt_ref, cwx_ref, cwb_ref, cwc_ref, cbx_ref, cbb_ref, cbc_ref,
                  pr_ref, dsk_ref, nw_ref, y_ref, hout_ref,
                  h_ref, tx_ref, tb_ref, tc_ref, ex_ref, eb_ref, ec_ref):
    c = pl.program_id(1)
    q = x_ref.shape[0]
    gw = x_ref.shape[1]

    @pl.when(c == 0)
    def _():
        h_ref[...] = jnp.zeros(h_ref.shape, F32)
        tx_ref[...] = jnp.zeros(tx_ref.shape, F32)
        tb_ref[...] = jnp.zeros(tb_ref.shape, F32)
        tc_ref[...] = jnp.zeros(tc_ref.shape, F32)

    xs = _conv_silu(x_ref, tx_ref, ex_ref, cwx_ref, cbx_ref)
    bm = _conv_silu(b_ref, tb_ref, eb_ref, cwb_ref, cbb_ref)
    cm = _conv_silu(c_ref, tc_ref, ec_ref, cwc_ref, cbc_ref)

    dt = _softplus(dt_ref[...] + pr_ref[0:1])
    a = -jnp.exp(pr_ref[1:2])
    dta = dt * a

    ri = lax.broadcasted_iota(jnp.int32, (q, q), 0)
    ci = lax.broadcasted_iota(jnp.int32, (q, q), 1)
    causal = ci <= ri
    tri = causal.astype(BF16)
    hi, mid, lo = _split3(dta)
    cum = _dot(tri, hi) + _dot(tri, mid) + _dot(tri, lo)
    cum_t = cum.T

    er = lax.broadcasted_iota(jnp.int32, (LANES, gw), 0)
    el = lax.broadcasted_iota(jnp.int32, (LANES, gw), 1)
    expand = (el // SSM_HEAD_DIM == er).astype(BF16)
    dt_e = _dot_exact_rhs(dt, expand)
    cum_e = _dot_exact_rhs(cum, expand)
    exp_cum_e = jnp.exp(cum_e)
    cum_last_e = cum_e[q - 1:q]
    decay_end_e = jnp.exp(cum_last_e - cum_e)

    xdt = xs * dt_e
    xdt_b = xdt.astype(BF16)
    bm_b = bm.astype(BF16)
    cm_b = cm.astype(BF16)
    cb = _dot_nt(cm_b, bm_b)

    lane = lax.broadcasted_iota(jnp.int32, (q, LANES), 1)
    y_parts = []
    for pp in range(SSM_HPG // 2):
        xp = xdt_b[:, pp * LANES:(pp + 1) * LANES]
        ys = []
        for r in (2 * pp, 2 * pp + 1):
            seg = cum[:, r:r + 1] - cum_t[r:r + 1, :]
            lmat = jnp.exp(jnp.where(causal, seg, NEG_INF))
            ys.append(_dot((cb * lmat).astype(BF16), xp))
        y_parts.append(jnp.where(lane < SSM_HEAD_DIM, ys[0], ys[1]))
    y_diag = jnp.concatenate(y_parts, axis=1)

    h_prev = h_ref[...]
    y_off = _dot(cm_b, h_prev.astype(BF16)) * exp_cum_e
    states = _dot(bm.T.astype(BF16), (xdt * decay_end_e).astype(BF16))
    h_new = h_prev * jnp.exp(cum_last_e) + states
    h_ref[...] = h_new

    y = y_diag + y_off + dsk_ref[...] * xs
    y = y * _silu(z_ref[...])
    y_ref[...] = _rms(y) * nw_ref[...]

    @pl.when(c == pl.num_programs(1) - 1)
    def _():
        hout_ref[...] = h_new


def _ssd_prompt(zx, conv_w, conv_b, params, d_skip_e, norm_w):
    m = zx.shape[0]
    q = min(SSM_CHUNK, m)
    gw = SSM_GROUP_W
    xcol = SSM_INNER // gw
    bcol = 2 * SSM_INNER // SSM_STATE
    ccol = bcol + SSM_GROUPS
    dcol = ccol + SSM_GROUPS
    tail = SUBLANES
    cw2 = conv_w
    cb2 = conv_b.reshape(1, -1)
    in_specs = [
        pl.BlockSpec((q, gw), lambda g, c: (c, xcol + g)),
        pl.BlockSpec((q, SSM_STATE), lambda g, c: (c, bcol + g)),
        pl.BlockSpec((q, SSM_STATE), lambda g, c: (c, ccol + g)),
        pl.BlockSpec((q, gw), lambda g, c: (c, g)),
        pl.BlockSpec((q, LANES), lambda g, c: (c, dcol + g)),
        pl.BlockSpec((SSM_CONV, gw), lambda g, c: (0, g)),
        pl.BlockSpec((SSM_CONV, SSM_STATE), lambda g, c: (0, SSM_INNER // SSM_STATE + g)),
        pl.BlockSpec((SSM_CONV, SSM_STATE), lambda g, c: (0, SSM_INNER // SSM_STATE + SSM_GROUPS + g)),
        pl.BlockSpec((1, gw), lambda g, c: (0, g)),
        pl.BlockSpec((1, SSM_STATE), lambda g, c: (0, SSM_INNER // SSM_STATE + g)),
        pl.BlockSpec((1, SSM_STATE), lambda g, c: (0, SSM_INNER // SSM_STATE + SSM_GROUPS + g)),
        pl.BlockSpec((None, SUBLANES, LANES), lambda g, c: (g, 0, 0)),
        pl.BlockSpec((1, gw), lambda g, c: (0, g)),
        pl.BlockSpec((1, gw), lambda g, c: (0, g)),
    ]
    return pl.pallas_call(
        _ssd_p_kernel,
        grid=(SSM_GROUPS, m // q),
        in_specs=in_specs,
        out_specs=[pl.BlockSpec((q, gw), lambda g, c: (c, g)),
                   pl.BlockSpec((None, SSM_STATE, gw), lambda g, c: (g, 0, 0))],
        out_shape=[jax.ShapeDtypeStruct((m, SSM_INNER), F32),
                   jax.ShapeDtypeStruct((SSM_GROUPS, SSM_STATE, gw), F32)],
        scratch_shapes=[pltpu.VMEM((SSM_STATE, gw), F32),
                        pltpu.VMEM((tail, gw), F32), pltpu.VMEM((tail, SSM_STATE), F32),
                        pltpu.VMEM((tail, SSM_STATE), F32),
                        pltpu.VMEM((tail + q, gw), F32), pltpu.VMEM((tail + q, SSM_STATE), F32),
                        pltpu.VMEM((tail + q, SSM_STATE), F32)],
        compiler_params=_params(("arbitrary", "arbitrary")),
        name="ssd_prompt",
    )(zx, zx, zx, zx, zx, cw2, cw2, cw2, cb2, cb2, cb2, params, d_skip_e.reshape(1, -1), norm_w.reshape(1, -1))


def _ssd_s_pre_kernel(xbc_ref, hist_ref, cw_ref, cb_ref, dt_ref, dtb_ref, alog_ref, xs_ref, bc_ref, xdt_ref, dec_ref):
    out = cb_ref[...] + xbc_ref[...] * cw_ref[SSM_CONV - 1:SSM_CONV]
    for tau in range(SSM_CONV - 1):
        out = out + hist_ref[tau] * cw_ref[tau:tau + 1]
    act = _silu(out)
    xs = act[:, :SSM_INNER]
    dt = _softplus(dt_ref[...] + dtb_ref[...])
    xs_ref[...] = xs
    bc_ref[...] = act[:, SSM_INNER:]
    xdt_ref[...] = xs * dt
    dec_ref[...] = jnp.exp(dt * -jnp.exp(alog_ref[...]))


def _ssd_decode_pre(xbc, hist_t, conv_w, conv_b, dt_e, dtb_e, alog_e):
    b = xbc.shape[0]
    full = lambda a: pl.BlockSpec(a.shape, lambda i: (0,) * a.ndim)
    args = (xbc, hist_t, conv_w, conv_b.reshape(1, -1), dt_e, dtb_e.reshape(1, -1), alog_e.reshape(1, -1))
    shapes = [jax.ShapeDtypeStruct((b, SSM_INNER), F32), jax.ShapeDtypeStruct((b, 2 * SSM_BC_W), F32),
              jax.ShapeDtypeStruct((b, SSM_INNER), F32), jax.ShapeDtypeStruct((b, SSM_INNER), F32)]
    return pl.pallas_call(
        _ssd_s_pre_kernel,
        grid=(1,),
        in_specs=[full(a) for a in args],
        out_specs=[pl.BlockSpec(s.shape, lambda i: (0, 0)) for s in shapes],
        out_shape=shapes,
        compiler_params=_params(("arbitrary",)),
        name="ssd_decode_pre",
    )(*args)


def _ssd_s_state_kernel(h_ref, xdt_ref, dec_ref, b_ref, c_ref, ho_ref, y_ref):
    nblk = xdt_ref.shape[1]
    rows_per_group = SSM_GROUP_W
    lane = lax.broadcasted_iota(jnp.int32, y_ref.shape, 1)
    y = jnp.zeros(y_ref.shape, F32)
    for blk in range(nblk):
        g = (blk * LANES) // rows_per_group
        rs = slice(blk * LANES, (blk + 1) * LANES)
        h_new = (h_ref[rs, :] * dec_ref[:, blk:blk + 1]
                 + xdt_ref[:, blk:blk + 1] * b_ref[:, g * SSM_STATE:(g + 1) * SSM_STATE])
        ho_ref[rs, :] = h_new
        col = jnp.sum(h_new * c_ref[:, g * SSM_STATE:(g + 1) * SSM_STATE], axis=1, keepdims=True)
        y = jnp.where(lane == blk, col, y)
    y_ref[...] = y


def _ssd_decode_state(h0, xdt_t, dec_t, bm, cm):
    b, hp, n = h0.shape
    nblk = hp // LANES
    return pl.pallas_call(
        _ssd_s_state_kernel,
        grid=(b,),
        in_specs=[
            pl.BlockSpec((None, hp, n), lambda i: (i, 0, 0)),
            pl.BlockSpec((None, LANES, nblk), lambda i: (i, 0, 0)),
            pl.BlockSpec((None, LANES, nblk), lambda i: (i, 0, 0)),
            pl.BlockSpec((None, 1, SSM_BC_W), lambda i: (i, 0, 0)),
            pl.BlockSpec((None, 1, SSM_BC_W), lambda i: (i, 0, 0)),
        ],
        out_specs=[pl.BlockSpec((None, hp, n), lambda i: (i, 0, 0)),
                   pl.BlockSpec((None, LANES, nblk), lambda i: (i, 0, 0))],
        out_shape=[jax.ShapeDtypeStruct((b, hp, n), F32), jax.ShapeDtypeStruct((b, LANES, nblk), F32)],
        compiler_params=_params(("arbitrary",)),
        name="ssd_decode_state",
    )(h0, xdt_t, dec_t, bm, cm)


def _ssd_s_post_kernel(y_ref, xs_ref, z_ref, dsk_ref, nw_ref, o_ref):
    y = (y_ref[...] + dsk_ref[...] * xs_ref[...]) * _silu(z_ref[...])
    for g in range(SSM_GROUPS):
        cs = slice(g * SSM_GROUP_W, (g + 1) * SSM_GROUP_W)
        o_ref[:, cs] = _rms(y[:, cs]) * nw_ref[:, cs]


def _ssd_decode_post(y, xs, z, d_skip_e, norm_w):
    args = (y, xs, z, d_skip_e.reshape(1, -1), norm_w.reshape(1, -1))
    return pl.pallas_call(
        _ssd_s_post_kernel,
        grid=(1,),
        in_specs=[pl.BlockSpec(a.shape, lambda i: (0, 0)) for a in args],
        out_specs=pl.BlockSpec(y.shape, lambda i: (0, 0)),
        out_shape=jax.ShapeDtypeStruct(y.shape, F32),
        compiler_params=_params(("arbitrary",)),
        name="ssd_decode_post",
    )(*args)


def _route_rows(x, sc, sh, wr_ref, br_ref, hn_ref, ti_ref, tp_ref):
    rows = x.shape[0]
    hn = _rms(x) * (1.0 + sc) + sh
    half = hn.shape[1] // 2
    lo = lax.bitcast_convert_type(hn[:, :half].astype(BF16).astype(F32), jnp.int32)
    hi = lax.bitcast_convert_type(hn[:, half:].astype(BF16).astype(F32), jnp.int32)
    hn_ref[0:rows] = lax.shift_right_logical(lo, 16) | (hi & HI_HALF_MASK)
    ah, am, al = _split3(hn)
    wh, wm, wl = _split3(wr_ref[...])
    lg = (_dot(ah, wh) + _dot(ah, wm) + _dot(am, wh) + _dot(ah, wl) + _dot(al, wh) + _dot(am, wm)) + br_ref[...]
    lane = lax.broadcasted_iota(jnp.int32, lg.shape, 1)
    lg = jnp.where(lane < N_EXPERTS, lg, NEG_INF)
    vals, idxs = [], []
    for _ in range(TOP_K):
        mx = jnp.max(lg, axis=1, keepdims=True)
        idx = jnp.min(jnp.where(lg == mx, lane, LANES), axis=1, keepdims=True)
        vals.append(mx)
        idxs.append(idx)
        lg = jnp.where(lane == idx, NEG_INF, lg)
    es = [jnp.exp(v - vals[0]) for v in vals]
    tot = es[0] + es[1] + es[2] + es[3]
    ti = jnp.zeros(lg.shape, jnp.int32)
    tp = jnp.zeros(lg.shape, F32)
    for k in range(TOP_K):
        ti = jnp.where(lane == k, idxs[k], ti)
        tp = jnp.where(lane == k, es[k] / tot, tp)
    ti_ref[0:rows] = ti
    tp_ref[0:rows] = tp


def _router_kernel(xp_ref, scp_ref, shp_ref, xs_ref, scs_ref, shs_ref, wr_ref, br_ref, hn_ref, ti_ref, tp_ref):
    last = pl.num_programs(0) - 1

    @pl.when(pl.program_id(0) < last)
    def _():
        _route_rows(xp_ref[...], scp_ref[...], shp_ref[...], wr_ref, br_ref, hn_ref, ti_ref, tp_ref)

    @pl.when(pl.program_id(0) == last)
    def _():
        _route_rows(xs_ref[...], scs_ref[...], shs_ref[...], wr_ref, br_ref, hn_ref, ti_ref, tp_ref)


def _norm_router(xp, sc_p, sh_p, xs, sc_s, sh_s, wr_pad, br_pad):
    mp, d = xp.shape
    nb = xs.shape[0]
    tm = min(TM, mp)
    assert mp % tm == 0 and nb <= tm
    np_ = mp // tm
    t_all = mp + nb
    row = pl.BlockSpec((1, d), lambda i: (0, 0))
    dec = pl.BlockSpec((nb, d), lambda i: (0, 0))
    return pl.pallas_call(
        _router_kernel,
        grid=(np_ + 1,),
        in_specs=[pl.BlockSpec((tm, d), lambda i: (jnp.minimum(i, np_ - 1), 0)), row, row, dec, dec, dec,
                  pl.BlockSpec((d, LANES), lambda i: (0, 0)), pl.BlockSpec((1, LANES), lambda i: (0, 0))],
        out_specs=[pl.BlockSpec((tm, d // 2), lambda i: (i, 0)),
                   pl.BlockSpec((tm, LANES), lambda i: (i, 0)),
                   pl.BlockSpec((tm, LANES), lambda i: (i, 0))],
        out_shape=[jax.ShapeDtypeStruct((t_all, d // 2), jnp.int32),
                   jax.ShapeDtypeStruct((t_all, LANES), jnp.int32),
                   jax.ShapeDtypeStruct((t_all, LANES), F32)],
        compiler_params=_params(("arbitrary",)),
        name="norm_router",
    )(xp, sc_p, sh_p, xs, sc_s, sh_s, wr_pad, br_pad)


def _route_plan(top_i, tm, n_tiles):
    t = top_i.shape[0]
    e_flat = top_i.reshape(-1)
    onehot = (e_flat[:, None] == jnp.arange(N_EXPERTS, dtype=jnp.int32)[None, :]).astype(jnp.int32)
    csum = jnp.cumsum(onehot, axis=0)
    rank = jnp.take_along_axis(csum, e_flat[:, None], axis=1)[:, 0] - 1
    cnt = csum[-1]
    tiles_e = (cnt + tm - 1) // tm
    tiles_end = jnp.cumsum(tiles_e)
    tile_start = tiles_end - tiles_e
    dest = (tile_start[e_flat] * tm + rank).astype(jnp.int32)
    total = tiles_end[-1]
    tile_ids = jnp.arange(n_tiles, dtype=jnp.int32)
    tile_valid = (tile_ids < total).astype(jnp.int32)
    tile_expert = jnp.sum((tiles_end[None, :] <= tile_ids[:, None]).astype(jnp.int32), axis=1)
    tile_expert = jnp.minimum(tile_expert, N_EXPERTS - 1)
    last_expert = tile_expert[jnp.maximum(total - 1, 0)]
    tile_expert = jnp.where(tile_valid == 1, tile_expert, last_expert)
    tile_src = jnp.minimum(tile_ids, jnp.maximum(total - 1, 0)).astype(jnp.int32)
    return dest, tile_expert.astype(jnp.int32), tile_valid, tile_src


def _dispatch_kernel(dest_ref, hn_hbm, xs_in, xs_hbm, sem, *, tc):
    del xs_in
    base = pl.program_id(0) * tc

    def body(j, carry):
        t = base + j
        for k in range(TOP_K):
            row = dest_ref[t * TOP_K + k]
            pltpu.make_async_copy(hn_hbm.at[pl.ds(t, 1)], xs_hbm.at[pl.ds(row, 1)], sem).start()
        return carry

    lax.fori_loop(0, tc, body, 0)
    pltpu.make_async_copy(hn_hbm.at[pl.ds(0, tc * TOP_K)], xs_hbm.at[pl.ds(0, tc * TOP_K)], sem).wait()


def _moe_dispatch(dest, hn_all, n_rows):
    t, w = hn_all.shape
    nchunk = math.gcd(t, 8)
    tc = t // nchunk
    grid_spec = pltpu.PrefetchScalarGridSpec(
        num_scalar_prefetch=1,
        grid=(nchunk,),
        in_specs=[pl.BlockSpec(memory_space=pl.ANY), pl.BlockSpec(memory_space=pl.ANY)],
        out_specs=pl.BlockSpec(memory_space=pl.ANY),
        scratch_shapes=[pltpu.SemaphoreType.DMA(())],
    )
    return pl.pallas_call(
        functools.partial(_dispatch_kernel, tc=tc),
        grid_spec=grid_spec,
        out_shape=jax.ShapeDtypeStruct((n_rows, w), hn_all.dtype),
        input_output_aliases={2: 0},
        compiler_params=_params(("arbitrary",)),
        name="moe_dispatch",
    )(dest, hn_all, jnp.zeros((n_rows, w), hn_all.dtype))


def _moe_kernel(te_ref, tv_ref, ts_ref, xs_ref, sel_ref, wgu_ref, bgu_ref, wdn_ref, bdn_ref, o_ref, xb_ref, *, tn):
    r = pl.program_id(0)
    c = pl.program_id(1)
    del te_ref, ts_ref

    @pl.when((c == 0) & (tv_ref[r] == 0))
    def _():
        o_ref[...] = jnp.zeros(o_ref.shape, F32)

    @pl.when((c == 0) & (tv_ref[r] == 1))
    def _():
        packed = xs_ref[...]
        half = packed.shape[1]
        xb_ref[:, :half] = lax.bitcast_convert_type(lax.shift_left(packed, 16), F32).astype(BF16)
        xb_ref[:, half:] = lax.bitcast_convert_type(packed & HI_HALF_MASK, F32).astype(BF16)
        o_ref[...] = jnp.broadcast_to(bdn_ref[...], o_ref.shape)

    @pl.when(tv_ref[r] == 1)
    def _():
        sw = sel_ref.shape[0]
        gu = _dot(xb_ref[...], wgu_ref[...].astype(BF16)) + bgu_ref[...]
        up = jnp.concatenate([pltpu.roll(gu[:, j * LANES:(j + 1) * LANES], LANES - 1, 1)
                              for j in range(tn // LANES)], axis=1)
        gate = jnp.minimum(gu, SWIGLU_LIMIT)
        up = jnp.clip(up, -SWIGLU_LIMIT, SWIGLU_LIMIT)
        act = ((up + 1.0) * gate * jax.nn.sigmoid(SWIGLU_ALPHA * gate)).astype(BF16)
        parts = [_dot(act[:, k * sw:(k + 1) * sw], sel_ref[...]).astype(BF16) for k in range(tn // sw)]
        o_ref[...] += _dot(jnp.concatenate(parts, axis=1), wdn_ref[...].astype(BF16))


def _moe_tm(n_tokens):
    fair = -(-(n_tokens * TOP_K) // N_EXPERTS)
    return 16 * (-(-(fair * 14) // (25 * 16)))


def _moe_experts(xs, plan, w_gu, b_gu, w_dn, b_dn, layer, tm, n_tiles):
    _, tile_expert, tile_valid, tile_src = plan
    d = w_dn.shape[-1]
    f2 = w_gu.shape[-1]
    tn = MOE_TN
    ne = w_gu.shape[1]
    sw = MOE_SEL_W
    sel = (jnp.arange(sw)[:, None] == 2 * jnp.arange(sw // 2)[None, :]).astype(BF16)
    grid_spec = pltpu.PrefetchScalarGridSpec(
        num_scalar_prefetch=3,
        grid=(n_tiles, f2 // tn),
        in_specs=[
            pl.BlockSpec((tm, d // 2), lambda r, c, te, tv, ts: (ts[r], 0)),
            pl.BlockSpec((sw, sw // 2), lambda r, c, te, tv, ts: (0, 0)),
            pl.BlockSpec((None, None, d, tn), lambda r, c, te, tv, ts: (layer, te[r], 0, c)),
            pl.BlockSpec((None, None, 1, tn), lambda r, c, te, tv, ts: (layer, te[r], 0, c)),
            pl.BlockSpec((None, None, tn // 2, d), lambda r, c, te, tv, ts: (layer, te[r], c, 0)),
            pl.BlockSpec((None, None, 1, d), lambda r, c, te, tv, ts: (layer, te[r], 0, 0)),
        ],
        out_specs=pl.BlockSpec((tm, d), lambda r, c, te, tv, ts: (r, 0)),
        scratch_shapes=[pltpu.VMEM((tm, d), BF16)],
    )
    return pl.pallas_call(
        functools.partial(_moe_kernel, tn=tn),
        grid_spec=grid_spec,
        out_shape=jax.ShapeDtypeStruct((n_tiles * tm, d), F32),
        compiler_params=_params(("arbitrary", "arbitrary")),
        name="moe_experts",
    )(tile_expert, tile_valid, tile_src, xs, sel, w_gu,
      b_gu.reshape(b_gu.shape[0], ne, 1, f2), w_dn, b_dn.reshape(b_dn.shape[0], ne, 1, d))


def _combine_kernel(dest_ref, ys_hbm, tp_ref, x_ref, g_ref, fw_ref, o_ref, yb_ref, sem, *, tc, tok0, final):
    i = pl.program_id(0)
    nsteps = pl.num_programs(0)

    def start_gather(tile, s):
        base = (tok0 + tile * tc) * TOP_K

        def body(j, carry):
            for k in range(TOP_K):
                row = dest_ref[base + j * TOP_K + k]
                pltpu.make_async_copy(ys_hbm.at[pl.ds(row, 1)], yb_ref.at[s, k, pl.ds(j, 1)], sem.at[s]).start()
            return carry

        lax.fori_loop(0, tc, body, 0)

    slot = i % 2

    @pl.when(i == 0)
    def _():
        start_gather(0, 0)

    for k in range(TOP_K):
        pltpu.make_async_copy(ys_hbm.at[pl.ds(0, tc)], yb_ref.at[slot, k], sem.at[slot]).wait()

    @pl.when(i + 1 < nsteps)
    def _():
        start_gather(i + 1, 1 - slot)

    tp = tp_ref[...]
    acc = yb_ref[slot, 0] * tp[:, 0:1]
    for k in range(1, TOP_K):
        acc = acc + yb_ref[slot, k] * tp[:, k:k + 1]
    out = x_ref[...] + g_ref[...] * acc
    if final:
        out = _rms(out) * fw_ref[...]
    o_ref[...] = out


def _moe_combine(dest, ys, top_p, x, gate, final_w, tok0, final):
    m, d = x.shape
    tc = min(256, m)
    mg = gate.shape[0]
    blk0 = tok0 // tc
    grid_spec = pltpu.PrefetchScalarGridSpec(
        num_scalar_prefetch=1,
        grid=(m // tc,),
        in_specs=[
            pl.BlockSpec(memory_space=pl.ANY),
            pl.BlockSpec((tc, LANES), lambda i, dr: (blk0 + i, 0)),
            pl.BlockSpec((tc, d), lambda i, dr: (i, 0)),
            pl.BlockSpec((tc if mg > 1 else 1, d), (lambda i, dr: (i, 0)) if mg > 1 else (lambda i, dr: (0, 0))),
            pl.BlockSpec((1, d), lambda i, dr: (0, 0)),
        ],
        out_specs=pl.BlockSpec((tc, d), lambda i, dr: (i, 0)),
        scratch_shapes=[pltpu.VMEM((2, TOP_K, tc, d), F32), pltpu.SemaphoreType.DMA((2,))],
    )
    return pl.pallas_call(
        functools.partial(_combine_kernel, tc=tc, tok0=tok0, final=final),
        grid_spec=grid_spec,
        out_shape=jax.ShapeDtypeStruct((m, d), F32),
        compiler_params=_params(("arbitrary",)),
        name="moe_combine",
    )(dest, ys, top_p, x, gate, final_w.reshape(1, d))


def _rope_tables(pos):
    half = A_QK_DIM // 2
    inv = ROPE_THETA ** (-jnp.arange(half, dtype=F32) * 2.0 / A_QK_DIM)
    ang = pos.astype(F32)[:, None] * inv[None, :]
    cos, sin = jnp.cos(ang), jnp.sin(ang)
    cos_t = jnp.tile(cos, (1, LANES // half))
    sin_t = jnp.tile(jnp.concatenate([-sin, sin], axis=1), (1, LANES // A_QK_DIM))
    return cos_t, sin_t


def kernel(x_prompt, x_sample, cache_k, cache_v, state_pool, state_conv, state_ssm, page_table, c_prompt, c_sample,
           ada_w, ada_b, ab_w_in, ab_lambda, ab_subln, ab_pool_w, ab_pool_scale, ab_w_out,
           ssd_w_in, ssd_conv_w, ssd_conv_b, ssd_dt_bias, ssd_a_log, ssd_d, ssd_norm, ssd_w_out,
           router_w, router_b, moe_w_gu, moe_b_gu, moe_w_dn, moe_b_dn, final_norm):
    d = D_MODEL
    seq = x_prompt.shape[1]
    nb = x_sample.shape[0]
    n_pages = page_table.shape[1]
    past_len = n_pages * cache_k.shape[2]
    t_all = seq + nb
    moe_tm = _moe_tm(t_all)
    n_tiles = -(-(t_all * TOP_K) // moe_tm) + N_EXPERTS

    c_rows = SUBLANES * (-(-(1 + nb) // SUBLANES))
    c_all = jnp.concatenate([c_prompt, c_sample, jnp.zeros((c_rows - 1 - nb, d), F32)], axis=0)
    mod = _ada_mod(c_all, ada_w, ada_b).reshape(DEPTH, c_rows, N_MOD, d)

    def mods(layer, m):
        return mod[layer, 0:1, m], mod[layer, 1:1 + nb, m]

    xp = x_prompt.reshape(seq, d)
    xs_ = x_sample.reshape(nb, d)
    rope_p = _rope_tables(jnp.arange(seq))
    rope_s = _rope_tables(jnp.full((nb,), past_len))
    nl_c, n_phys, page = cache_k.shape[:3]
    cache_k4 = jnp.transpose(cache_k, (0, 1, 3, 4, 5, 2)).reshape(nl_c, n_phys, QK_WIDTH, page)
    cache_v4 = cache_v.reshape(nl_c, n_phys, page * A_HEADS, A_V_DIM)

    def moe_block(layer, xp, xs_, final):
        sh_p, sh_s = mods(layer, 3)
        sc_p, sc_s = mods(layer, 4)
        g_p, g_s = mods(layer, 5)
        wr = jnp.pad(router_w[layer], ((0, 0), (0, LANES - N_EXPERTS)))
        br = jnp.pad(router_b[layer], (0, LANES - N_EXPERTS)).reshape(1, LANES)
        hn_all, ti_all, tp_all = _norm_router(xp, sc_p, sh_p, xs_, sc_s, sh_s, wr, br)
        plan = _route_plan(ti_all[:, :TOP_K], moe_tm, n_tiles)
        xs_sorted = _moe_dispatch(plan[0], hn_all, n_tiles * moe_tm)
        ys = _moe_experts(xs_sorted, plan, moe_w_gu, moe_b_gu, moe_w_dn, moe_b_dn, layer, moe_tm, n_tiles)
        xp = _moe_combine(plan[0], ys, tp_all, xp, g_p, final_norm, 0, final)
        xs_ = _moe_combine(plan[0], ys, tp_all, xs_, g_s, final_norm, seq, final)
        return xp, xs_

    sh_p, sh_s = mods(0, 0)
    sc_p, sc_s = mods(0, 1)
    g_p, g_s = mods(0, 2)
    lam0 = _lambda_init(0)
    w_in0 = ab_w_in[0].astype(BF16)
    w_out0 = ab_w_out[0].astype(BF16)
    ucol = 2 * QK_WIDTH + A_WIDTH

    qkvu_p = _norm_matmul(xp, sc_p, sh_p, w_in0, 1024, rope_p, 2 * QK_WIDTH)
    o_p = _attn_prompt(qkvu_p, ab_lambda[0], ab_subln[0], lam0)
    pg_p = _pool_prompt(qkvu_p, ab_pool_w[0], ab_pool_scale[0])
    k_prompt = qkvu_p[:, QK_WIDTH:2 * QK_WIDTH].reshape(1, 1, seq, A_HEADS, 2, A_QK_DIM)
    v_prompt = qkvu_p[:, 2 * QK_WIDTH:ucol].reshape(1, 1, seq, A_HEADS, A_V_DIM)
    pool_prompt = qkvu_p[seq - POOL_HIST:, ucol:].reshape(1, 1, POOL_HIST, POOL_WIDTH)
    xp = _matmul_resid([o_p, pg_p], w_out0, xp, g_p)

    qkvu_s = _norm_matmul(xs_, sc_s, sh_s, w_in0, 1024, rope_s, 2 * QK_WIDTH)
    q_s = qkvu_s[:, :QK_WIDTH].reshape(nb, 1, QK_WIDTH)
    k_s = qkvu_s[:, QK_WIDTH:2 * QK_WIDTH].reshape(nb, 1, QK_WIDTH)
    v_s = qkvu_s[:, 2 * QK_WIDTH:ucol].reshape(nb, 1, A_WIDTH)
    u_s = qkvu_s[:, ucol:]
    o_s = _attn_decode(q_s, k_s, v_s, cache_k4, cache_v4, 0, page_table, ab_lambda[0], ab_subln[0], lam0)
    pg_s = _pool_decode(u_s, jnp.swapaxes(state_pool[0], 0, 1), ab_pool_w[0], ab_pool_scale[0])
    k_sample = k_s.reshape(1, nb, 1, A_HEADS, 2, A_QK_DIM)
    v_sample = v_s.reshape(1, nb, 1, A_HEADS, A_V_DIM)
    pool_sample = jnp.concatenate([state_pool[0][:, 1:], u_s[:, None, :]], axis=1)[None]
    xs_ = _matmul_resid([o_s.reshape(nb, A_WIDTH), pg_s], w_out0, xs_, g_s)

    xp, xs_ = moe_block(0, xp, xs_, False)

    sh_p, sh_s = mods(1, 0)
    sc_p, sc_s = mods(1, 1)
    g_p, g_s = mods(1, 2)
    zx_w = SSM_INNER + SSM_CONV_DIM
    w_dt = ssd_w_in[0][:, zx_w:].reshape(d, SSM_GROUPS, SSM_HPG)
    w_dt = jnp.pad(w_dt, ((0, 0), (0, 0), (0, LANES - SSM_HPG))).reshape(d, SSM_GROUPS * LANES)
    w_in1 = jnp.concatenate([ssd_w_in[0][:, :zx_w], w_dt], axis=1).astype(BF16)
    w_out1 = ssd_w_out[0].astype(BF16)
    tn1 = w_in1.shape[1] // SSM_GROUPS

    def per_group(v):
        return jnp.pad(v.reshape(SSM_GROUPS, SSM_HPG), ((0, 0), (0, LANES - SSM_HPG)))

    zeros_g = jnp.zeros((SSM_GROUPS, LANES), F32)
    params = jnp.stack([per_group(ssd_dt_bias[0]), per_group(ssd_a_log[0])] + [zeros_g] * (SUBLANES - 2), axis=1)
    dsk_e = jnp.repeat(ssd_d[0], SSM_HEAD_DIM)
    dtb_e = jnp.repeat(ssd_dt_bias[0], SSM_HEAD_DIM)
    alog_e = jnp.repeat(ssd_a_log[0], SSM_HEAD_DIM)

    zx_p = _norm_matmul(xp, sc_p, sh_p, w_in1, tn1)
    y_p, h_p = _ssd_prompt(zx_p, ssd_conv_w[0], ssd_conv_b[0], params, dsk_e, ssd_norm[0])
    conv_prompt = zx_p[seq - (SSM_CONV - 1):, SSM_INNER:zx_w].reshape(1, 1, SSM_CONV - 1, SSM_CONV_DIM)
    ssm_prompt = jnp.transpose(h_p.reshape(SSM_GROUPS, SSM_STATE, SSM_HPG, SSM_HEAD_DIM), (0, 2, 3, 1))
    ssm_prompt = ssm_prompt.reshape(1, 1, SSM_HEADS, SSM_HEAD_DIM, SSM_STATE)
    xp = _matmul_resid([y_p], w_out1, xp, g_p)

    zx_s = _norm_matmul(xs_, sc_s, sh_s, w_in1, tn1)
    z_s = zx_s[:, :SSM_INNER]
    xbc_s = zx_s[:, SSM_INNER:zx_w]
    dt_s = zx_s[:, zx_w:].reshape(nb, SSM_GROUPS, LANES)[:, :, :SSM_HPG].reshape(nb, SSM_HEADS)
    dt_e = jnp.repeat(dt_s, SSM_HEAD_DIM, axis=1)
    xs_act, bc_s, xdt_s, dec_s = _ssd_decode_pre(xbc_s, jnp.swapaxes(state_conv[0], 0, 1), ssd_conv_w[0],
                                                 ssd_conv_b[0], dt_e, dtb_e, alog_e)
    nblk = SSM_INNER // LANES

    def col_major(v):
        return jnp.swapaxes(v.reshape(nb, nblk, LANES), 1, 2)

    h_s, y_t = _ssd_decode_state(state_ssm[0].reshape(nb, SSM_INNER, SSM_STATE), col_major(xdt_s), col_major(dec_s),
                                 bc_s[:, :SSM_BC_W].reshape(nb, 1, SSM_BC_W), bc_s[:, SSM_BC_W:].reshape(nb, 1, SSM_BC_W))
    y_s = jnp.swapaxes(y_t, 1, 2).reshape(nb, SSM_INNER)
    yn_s = _ssd_decode_post(y_s, xs_act, z_s, dsk_e, ssd_norm[0])
    conv_sample = jnp.concatenate([state_conv[0][:, 1:], xbc_s[:, None, :]], axis=1)[None]
    ssm_sample = h_s.reshape(1, nb, SSM_HEADS, SSM_HEAD_DIM, SSM_STATE)
    xs_ = _matmul_resid([yn_s], w_out1, xs_, g_s)

    xp, xs_ = moe_block(1, xp, xs_, True)

    return (xp.reshape(1, seq, d), xs_.reshape(nb, 1, d), k_prompt, v_prompt, pool_prompt, conv_prompt, ssm_prompt,
            k_sample, v_sample, pool_sample, conv_sample, ssm_sample)
```

```python
import functools
import math

import jax
import jax.numpy as jnp
from jax import lax
from jax.experimental import pallas as pl
from jax.experimental.pallas import tpu as pltpu

F32 = jnp.float32
BF16 = jnp.bfloat16
NEG_INF = float("-inf")
LOG2_E = math.log2(math.e)

D_MODEL = 2048
DEPTH = 2
PAGE_SIZE = 128
NORM_EPS = 1e-5
N_MOD = 6
A_HEADS = 8
A_QK_DIM = 64
A_V_DIM = 128
QK_WIDTH = A_HEADS * 2 * A_QK_DIM
A_WIDTH = A_HEADS * A_V_DIM
A_SCALE = A_QK_DIM ** -0.5
ROPE_THETA = 10000.0
POOL_WINDOWS = (2, 4, 8, 16)
POOL_WIDTH = D_MODEL // 2
POOL_GROUP_DIM = POOL_WIDTH // 4
POOL_HIST = 15
SSM_INNER = 2 * D_MODEL
SSM_HEAD_DIM = 64
SSM_HEADS = SSM_INNER // SSM_HEAD_DIM
SSM_GROUPS = 8
SSM_HPG = SSM_HEADS // SSM_GROUPS
SSM_STATE = 128
SSM_CONV = 4
SSM_CHUNK = 128
SSM_GROUP_W = SSM_INNER // SSM_GROUPS
SSM_BC_W = SSM_GROUPS * SSM_STATE
SSM_CONV_DIM = SSM_INNER + 2 * SSM_BC_W
N_EXPERTS = 32
TOP_K = 4
SWIGLU_ALPHA = 1.702
SWIGLU_LIMIT = 7.0

LANES = 128
SUBLANES = 8
VMEM_LIMIT = 56 * 1024 * 1024

TM = 512
NORM_MM_TM = 1024
DISPATCH_MAX_ROWS = 2056
MOE_TN = 1024
MOE_SEL_W = 512
ATT_TQ = 1024
ATT_TK = 512
DEC_PPS = 8


def _params(sem):
    return pltpu.CompilerParams(dimension_semantics=sem, vmem_limit_bytes=VMEM_LIMIT)


def _lambda_init(layer):
    return 0.8 - 0.6 * math.exp(-0.3 * layer)


def _split3(x):
    hi = x.astype(BF16)
    r1 = x - hi.astype(F32)
    mid = r1.astype(BF16)
    lo = (r1 - mid.astype(F32)).astype(BF16)
    return hi, mid, lo


def _dot(a, b):
    return jnp.dot(a, b, preferred_element_type=F32)


def _dot_nt(a, b):
    return lax.dot_general(a, b, (((1,), (1,)), ((), ())), preferred_element_type=F32)


def _dot_exact_rhs(x, sel):
    hi, mid, lo = _split3(x)
    return _dot(hi, sel) + _dot(mid, sel) + _dot(lo, sel)


def _silu(x):
    return x * jax.nn.sigmoid(x)


def _rms(x):
    return x * lax.rsqrt(jnp.mean(x * x, axis=-1, keepdims=True) + NORM_EPS)


def _ada_kernel(c_ref, w_ref, b_ref, o_ref):
    a = _silu(c_ref[...]).astype(BF16)
    o_ref[...] = _dot(a, w_ref[...].astype(BF16)) + b_ref[...]


def _ada_mod(c_all, ada_w, ada_b, tn=1024):
    m, d = c_all.shape
    nl, _, n = ada_w.shape
    return pl.pallas_call(
        _ada_kernel,
        grid=(nl, n // tn),
        in_specs=[
            pl.BlockSpec((m, d), lambda l, j: (0, 0)),
            pl.BlockSpec((None, d, tn), lambda l, j: (l, 0, j)),
            pl.BlockSpec((None, 1, tn), lambda l, j: (l, 0, j)),
        ],
        out_specs=pl.BlockSpec((None, m, tn), lambda l, j: (l, 0, j)),
        out_shape=jax.ShapeDtypeStruct((nl, m, n), F32),
        compiler_params=_params(("arbitrary", "arbitrary")),
        name="ada_mod",
    )(c_all, ada_w, ada_b.reshape(nl, 1, n))


def _norm_mm_kernel(*refs, rope_tiles, tn):
    if rope_tiles:
        x_ref, sc_ref, sh_ref, w_ref, cos_ref, sin_ref, o_ref, hn_ref = refs
    else:
        x_ref, sc_ref, sh_ref, w_ref, o_ref, hn_ref = refs
    j = pl.program_id(1)

    @pl.when(j == 0)
    def _():
        hn_ref[...] = (_rms(x_ref[...]) * (1.0 + sc_ref[...]) + sh_ref[...]).astype(BF16)

    y = _dot(hn_ref[...], w_ref[...])
    if not rope_tiles:
        o_ref[...] = y
        return

    @pl.when(j >= rope_tiles)
    def _():
        o_ref[...] = y

    @pl.when(j < rope_tiles)
    def _():
        cos = cos_ref[...]
        sin = sin_ref[...]
        lane = lax.broadcasted_iota(jnp.int32, cos.shape, 1)
        first = (lane % A_QK_DIM) < (A_QK_DIM // 2)
        for c in range(tn // LANES):
            yc = y[:, c * LANES:(c + 1) * LANES]
            partner = jnp.where(first, pltpu.roll(yc, LANES - A_QK_DIM // 2, 1),
                                pltpu.roll(yc, A_QK_DIM // 2, 1))
            o_ref[:, c * LANES:(c + 1) * LANES] = yc * cos + partner * sin


def _norm_matmul(x, scale, shift, w, tn, rope=None, rope_cols=0):
    m, d = x.shape
    n = w.shape[1]
    tm = min(NORM_MM_TM, m)
    ms = scale.shape[0]
    mod_spec = pl.BlockSpec((tm if ms > 1 else 1, d), (lambda i, j: (i, 0)) if ms > 1 else (lambda i, j: (0, 0)))
    in_specs = [pl.BlockSpec((tm, d), lambda i, j: (i, 0)), mod_spec, mod_spec,
                pl.BlockSpec((d, tn), lambda i, j: (0, j))]
    args = [x, scale, shift, w]
    if rope is not None:
        in_specs += [pl.BlockSpec((tm, LANES), lambda i, j: (i, 0))] * 2
        args += list(rope)
    return pl.pallas_call(
        functools.partial(_norm_mm_kernel, rope_tiles=rope_cols // tn, tn=tn),
        grid=(m // tm, n // tn),
        in_specs=in_specs,
        out_specs=pl.BlockSpec((tm, tn), lambda i, j: (i, j)),
        out_shape=jax.ShapeDtypeStruct((m, n), F32),
        scratch_shapes=[pltpu.VMEM((tm, d), BF16)],
        compiler_params=_params(("arbitrary", "arbitrary")),
        name="norm_matmul",
    )(*args)


def _mm_resid_kernel(*refs, n_a):
    a_refs = refs[:n_a]
    w_ref, x_ref, g_ref, o_ref, ab_ref = refs[n_a:]

    @pl.when(pl.program_id(1) == 0)
    def _():
        off = 0
        for a_ref in a_refs:
            ka = a_ref.shape[1]
            ab_ref[:, off:off + ka] = a_ref[...].astype(BF16)
            off += ka

    o_ref[...] = x_ref[...] + g_ref[...] * _dot(ab_ref[...], w_ref[...])


def _matmul_resid(a_list, w, x, gate, tn=1024):
    m, n = x.shape
    k = w.shape[0]
    tm = min(TM, m)
    mg = gate.shape[0]
    in_specs = [pl.BlockSpec((tm, a.shape[1]), lambda i, j: (i, 0)) for a in a_list]
    in_specs += [
        pl.BlockSpec((k, tn), lambda i, j: (0, j)),
        pl.BlockSpec((tm, tn), lambda i, j: (i, j)),
        pl.BlockSpec((tm if mg > 1 else 1, tn), (lambda i, j: (i, j)) if mg > 1 else (lambda i, j: (0, j))),
    ]
    return pl.pallas_call(
        functools.partial(_mm_resid_kernel, n_a=len(a_list)),
        grid=(m // tm, n // tn),
        in_specs=in_specs,
        out_specs=pl.BlockSpec((tm, tn), lambda i, j: (i, j)),
        out_shape=jax.ShapeDtypeStruct((m, n), F32),
        scratch_shapes=[pltpu.VMEM((tm, k), BF16)],
        compiler_params=_params(("arbitrary", "arbitrary")),
        name="matmul_resid",
    )(*a_list, w, x, gate)


def _lam_value(lam_ref, lam_init):
    lv = lam_ref[...]
    s01 = jnp.sum(lv[0:1] * lv[1:2], axis=1, keepdims=True)
    s23 = jnp.sum(lv[2:3] * lv[3:4], axis=1, keepdims=True)
    return jnp.exp(s01) - jnp.exp(s23) + lam_init


def _attn_p_kernel(qi_ref, ki_ref, q_ref, k_ref, v_ref, lam_ref, sub_ref, o_ref, m_ref, l_ref, acc_ref, *, lam_init):
    p = pl.program_id(1)
    qi = qi_ref[p]
    ki = ki_ref[p]
    tq = q_ref.shape[0]
    tk = k_ref.shape[0]
    ratio = tq // tk
    shift = ki * tk - qi * tq

    @pl.when(ki == 0)
    def _():
        m_ref[...] = jnp.full(m_ref.shape, NEG_INF, F32)
        l_ref[...] = jnp.zeros(l_ref.shape, F32)
        acc_ref[...] = jnp.zeros(acc_ref.shape, F32)

    def step(masked):
        q = q_ref[...] * (A_SCALE * LOG2_E)
        lane = lax.broadcasted_iota(jnp.int32, q.shape, 1)
        kb = k_ref[...].astype(BF16)
        vb = v_ref[...].astype(BF16)
        if masked:
            row = lax.broadcasted_iota(jnp.int32, (tq, LANES), 0)
            col = lax.broadcasted_iota(jnp.int32, (tq, LANES), 1) + shift
        for c in range(2):
            qc = jnp.where((lane < A_QK_DIM) == (c == 0), q, 0.0).astype(BF16)
            s = _dot_nt(qc, kb)
            chunks = [s[:, j * LANES:(j + 1) * LANES] for j in range(tk // LANES)]
            if masked:
                chunks = [jnp.where(col + j * LANES <= row, ch, NEG_INF) for j, ch in enumerate(chunks)]
            mx = chunks[0]
            for ch in chunks[1:]:
                mx = jnp.maximum(mx, ch)
            m_prev = m_ref[c]
            m_new = jnp.maximum(m_prev, jnp.max(mx, axis=1, keepdims=True))
            alpha = jnp.exp2(m_prev - m_new)
            es = [jnp.exp2(ch - m_new) for ch in chunks]
            tot = es[0]
            for e in es[1:]:
                tot = tot + e
            l_ref[c] = alpha * l_ref[c] + tot
            acc_ref[c] = alpha * acc_ref[c] + _dot(jnp.concatenate(es, axis=1).astype(BF16), vb)
            m_ref[c] = m_new

    @pl.when(ki < ratio * qi)
    def _():
        step(False)

    @pl.when(ki >= ratio * qi)
    def _():
        step(True)

    @pl.when(ki == ratio * (qi + 1) - 1)
    def _():
        lam = _lam_value(lam_ref, lam_init)
        l0 = jnp.sum(l_ref[0], axis=1, keepdims=True)
        l1 = jnp.sum(l_ref[1], axis=1, keepdims=True)
        o = acc_ref[0] / l0 - lam * (acc_ref[1] / l1)
        o_ref[...] = _rms(o) * sub_ref[...] * (1.0 - lam_init)


def _attn_prompt(qkvu, lam_vec, subln, lam_init):
    m = qkvu.shape[0]
    tk = min(ATT_TK, m)
    tq = min(ATT_TQ, m)
    assert m % tq == 0 and tq % tk == 0
    ratio = tq // tk
    pairs = [(qi, ki) for qi in range(m // tq) for ki in range(ratio * (qi + 1))]
    qi_tab = jnp.asarray([p[0] for p in pairs], jnp.int32)
    ki_tab = jnp.asarray([p[1] for p in pairs], jnp.int32)
    kcol = QK_WIDTH // LANES
    vcol = 2 * QK_WIDTH // LANES
    grid_spec = pltpu.PrefetchScalarGridSpec(
        num_scalar_prefetch=2,
        grid=(A_HEADS, len(pairs)),
        in_specs=[
            pl.BlockSpec((tq, LANES), lambda h, p, qt, kt: (qt[p], h)),
            pl.BlockSpec((tk, LANES), lambda h, p, qt, kt: (kt[p], kcol + h)),
            pl.BlockSpec((tk, LANES), lambda h, p, qt, kt: (kt[p], vcol + h)),
            pl.BlockSpec((4, A_QK_DIM), lambda h, p, qt, kt: (0, 0)),
            pl.BlockSpec((1, A_V_DIM), lambda h, p, qt, kt: (0, 0)),
        ],
        out_specs=pl.BlockSpec((tq, LANES), lambda h, p, qt, kt: (qt[p], h)),
        scratch_shapes=[pltpu.VMEM((2, tq, LANES), F32)] * 3,
    )
    return pl.pallas_call(
        functools.partial(_attn_p_kernel, lam_init=lam_init),
        grid_spec=grid_spec,
        out_shape=jax.ShapeDtypeStruct((m, A_WIDTH), F32),
        compiler_params=_params(("arbitrary", "arbitrary")),
        name="attn_prompt",
    )(qi_tab, ki_tab, qkvu, qkvu, qkvu, lam_vec, subln.reshape(1, A_V_DIM))


def _attn_s_kernel(pt_ref, q_ref, kn_ref, vn_ref, lam_ref, sub_ref, rep_ref, *rest, pps, lam_init):
    kt_refs = rest[:pps]
    v_refs = rest[pps:2 * pps]
    o_ref, m_ref, l_ref, acc_ref = rest[2 * pps:]
    g = pl.program_id(1)
    nrow = 2 * A_HEADS
    row = lax.broadcasted_iota(jnp.int32, (nrow, QK_WIDTH), 0)
    lane = lax.broadcasted_iota(jnp.int32, (nrow, QK_WIDTH), 1)
    own = (lane // A_QK_DIM) == row
    qmat = jnp.where(own, q_ref[...] * A_SCALE, 0.0).astype(BF16)
    page = kt_refs[0].shape[1]

    @pl.when(g == 0)
    def _():
        kn = jnp.where(own, kn_ref[...], 0.0).astype(BF16)
        m_ref[...] = jnp.sum(qmat.astype(F32) * kn.astype(F32), axis=1, keepdims=True)
        l_ref[...] = jnp.ones(l_ref.shape, F32)
        vn = vn_ref[...].astype(BF16).astype(F32)
        for h in range(A_HEADS):
            acc_ref[2 * h:2 * h + 2, :] = jnp.broadcast_to(vn[:, h * A_V_DIM:(h + 1) * A_V_DIM], (2, A_V_DIM))

    s = jnp.concatenate([_dot(qmat, kt_refs[i][...].astype(BF16)) for i in range(pps)], axis=1)
    m_prev = m_ref[...]
    m_new = jnp.maximum(m_prev, jnp.max(s, axis=1, keepdims=True))
    alpha = jnp.exp(m_prev - m_new)
    e = jnp.exp(s - m_new)
    l_ref[...] = alpha * l_ref[...] + jnp.sum(e, axis=1, keepdims=True)
    m_ref[...] = m_new
    eb = jnp.concatenate([e[:, i * page:(i + 1) * page] for i in range(pps)], axis=0).astype(BF16)
    spread = _dot(eb, rep_ref[...])
    vrow = lax.broadcasted_iota(jnp.int32, (nrow, page * A_HEADS), 0)
    vlane = lax.broadcasted_iota(jnp.int32, (nrow, page * A_HEADS), 1)
    mine = (vlane % A_HEADS) == (vrow // 2)
    w = jnp.concatenate([jnp.where(mine, spread[i * nrow:(i + 1) * nrow], 0.0).astype(BF16) for i in range(pps)],
                        axis=1)
    v_all = jnp.concatenate([v_refs[i][...].astype(BF16) for i in range(pps)], axis=0)
    acc_ref[...] = alpha * acc_ref[...] + _dot(w, v_all)

    @pl.when(g == pl.num_programs(1) - 1)
    def _():
        lam = _lam_value(lam_ref, lam_init)
        r = acc_ref[...] / l_ref[...]
        for h in range(A_HEADS):
            o = r[2 * h:2 * h + 1] - lam * r[2 * h + 1:2 * h + 2]
            o_ref[:, h * A_V_DIM:(h + 1) * A_V_DIM] = _rms(o) * sub_ref[...] * (1.0 - lam_init)


def _attn_decode(q, k_new, v_new, cache_k, cache_v, layer, page_table, lam_vec, subln, lam_init):
    b = q.shape[0]
    n_pages = page_table.shape[1]
    pps = math.gcd(DEC_PPS, n_pages)
    page = cache_k.shape[3]
    row_spec = pl.BlockSpec((None, 1, QK_WIDTH), lambda bi, g, pt: (bi, 0, 0))
    rep = (jnp.arange(page * A_HEADS)[None, :] // A_HEADS == jnp.arange(page)[:, None]).astype(BF16)

    def page_spec(i, rows, cols):
        return pl.BlockSpec((None, None, rows, cols),
                            lambda bi, g, pt: (layer, pt[bi * n_pages + g * pps + i], 0, 0))

    grid_spec = pltpu.PrefetchScalarGridSpec(
        num_scalar_prefetch=1,
        grid=(b, n_pages // pps),
        in_specs=[row_spec, row_spec, row_spec,
                  pl.BlockSpec((4, A_QK_DIM), lambda bi, g, pt: (0, 0)),
                  pl.BlockSpec((1, A_V_DIM), lambda bi, g, pt: (0, 0)),
                  pl.BlockSpec((page, page * A_HEADS), lambda bi, g, pt: (0, 0))]
                 + [page_spec(i, QK_WIDTH, page) for i in range(pps)]
                 + [page_spec(i, page * A_HEADS, A_V_DIM) for i in range(pps)],
        out_specs=row_spec,
        scratch_shapes=[pltpu.VMEM((2 * A_HEADS, 1), F32), pltpu.VMEM((2 * A_HEADS, 1), F32),
                        pltpu.VMEM((2 * A_HEADS, A_V_DIM), F32)],
    )
    return pl.pallas_call(
        functools.partial(_attn_s_kernel, pps=pps, lam_init=lam_init),
        grid_spec=grid_spec,
        out_shape=jax.ShapeDtypeStruct((b, 1, A_WIDTH), F32),
        compiler_params=_params(("arbitrary", "arbitrary")),
        name="attn_decode",
    )(page_table.reshape(-1), q, k_new, v_new, lam_vec, subln.reshape(1, A_V_DIM), rep,
      *([cache_k] * pps), *([cache_v] * pps))


def _pool_project(pooled, pw_ref, ps_ref, o_ref):
    for g in range(len(POOL_WINDOWS)):
        cs = slice(g * POOL_GROUP_DIM, (g + 1) * POOL_GROUP_DIM)
        o_ref[:, cs] = _dot(pooled[g].astype(BF16), pw_ref[g].astype(BF16)) * ps_ref[:, cs]


def _pool_p_kernel(u_ref, halo_ref, pw_ref, ps_ref, o_ref, s_ref):
    i = pl.program_id(0)
    tm = u_ref.shape[0]
    hr = halo_ref.shape[0]
    s_ref[0:hr] = jnp.where(i > 0, halo_ref[...], 0.0)
    s_ref[hr:] = u_ref[...]
    rows = hr + tm
    d = 1
    while d < POOL_WINDOWS[-1]:
        c0 = sum(POOL_GROUP_DIM for w in POOL_WINDOWS if w <= d)
        s_ref[d:rows, c0:] = s_ref[d:rows, c0:] + s_ref[0:rows - d, c0:]
        d *= 2
    pos = i * tm + lax.broadcasted_iota(jnp.int32, (tm, 1), 0)
    pooled = []
    for g, w in enumerate(POOL_WINDOWS):
        cs = slice(g * POOL_GROUP_DIM, (g + 1) * POOL_GROUP_DIM)
        cnt = jnp.minimum(w, pos + 1).astype(F32)
        pooled.append(s_ref[hr:, cs] / cnt - u_ref[:, cs])
    _pool_project(pooled, pw_ref, ps_ref, o_ref)


def _pool_prompt(qkvu, pool_w, pool_scale):
    m = qkvu.shape[0]
    tm = min(TM, m)
    hr = 2 * SUBLANES
    ucol = (2 * QK_WIDTH + A_WIDTH) // POOL_WIDTH
    return pl.pallas_call(
        _pool_p_kernel,
        grid=(m // tm,),
        in_specs=[
            pl.BlockSpec((tm, POOL_WIDTH), lambda i: (i, ucol)),
            pl.BlockSpec((hr, POOL_WIDTH), lambda i: (jnp.maximum(i * (tm // hr) - 1, 0), ucol)),
            pl.BlockSpec(pool_w.shape, lambda i: (0, 0, 0)),
            pl.BlockSpec((1, POOL_WIDTH), lambda i: (0, 0)),
        ],
        out_specs=pl.BlockSpec((tm, POOL_WIDTH), lambda i: (i, 0)),
        out_shape=jax.ShapeDtypeStruct((m, POOL_WIDTH), F32),
        scratch_shapes=[pltpu.VMEM((hr + tm, POOL_WIDTH), F32)],
        compiler_params=_params(("arbitrary",)),
        name="pool_prompt",
    )(qkvu, qkvu, pool_w, pool_scale.reshape(1, POOL_WIDTH))


def _pool_s_kernel(u_ref, hist_ref, pw_ref, ps_ref, o_ref):
    u = u_ref[...]
    pooled = []
    for g, w in enumerate(POOL_WINDOWS):
        cs = slice(g * POOL_GROUP_DIM, (g + 1) * POOL_GROUP_DIM)
        tot = u[:, cs]
        for k in range(1, w):
            tot = tot + hist_ref[POOL_HIST - k, :, cs]
        pooled.append(tot / float(w) - u[:, cs])
    _pool_project(pooled, pw_ref, ps_ref, o_ref)


def _pool_decode(u, hist_t, pool_w, pool_scale):
    b = u.shape[0]
    return pl.pallas_call(
        _pool_s_kernel,
        grid=(1,),
        in_specs=[
            pl.BlockSpec(u.shape, lambda i: (0, 0)),
            pl.BlockSpec(hist_t.shape, lambda i: (0, 0, 0)),
            pl.BlockSpec(pool_w.shape, lambda i: (0, 0, 0)),
            pl.BlockSpec((1, POOL_WIDTH), lambda i: (0, 0)),
        ],
        out_specs=pl.BlockSpec((b, POOL_WIDTH), lambda i: (0, 0)),
        out_shape=jax.ShapeDtypeStruct((b, POOL_WIDTH), F32),
        compiler_params=_params(("arbitrary",)),
        name="pool_decode",
    )(u, hist_t, pool_w, pool_scale.reshape(1, POOL_WIDTH))


def _softplus(x):
    return jnp.maximum(x, 0.0) + jnp.log1p(jnp.exp(-jnp.abs(x)))


def _conv_silu(raw_ref, tail_ref, ext_ref, w_ref, b_ref):
    q = raw_ref.shape[0]
    t = tail_ref.shape[0]
    ext_ref[0:t] = tail_ref[...]
    ext_ref[t:] = raw_ref[...]
    tail_ref[...] = raw_ref[q - t:q]
    out = b_ref[...]
    for tau in range(SSM_CONV):
        off = t - (SSM_CONV - 1) + tau
        out = out + ext_ref[off:off + q] * w_ref[tau:tau + 1]
    return _silu(out)


def _ssd_p_kernel(x_ref, b_ref, c_ref, z_ref, dt_ref, cwx_ref, cwb_ref, cwc_ref, cbx_ref, cbb_ref, cbc_ref,
                  pr_ref, dsk_ref, nw_ref, y_ref, hout_ref,
                  h_ref, tx_ref, tb_ref, tc_ref, ex_ref, eb_ref, ec_ref):
    c = pl.program_id(1)
    q = x_ref.shape[0]
    gw = x_ref.shape[1]

    @pl.when(c == 0)
    def _():
        h_ref[...] = jnp.zeros(h_ref.shape, F32)
        tx_ref[...] = jnp.zeros(tx_ref.shape, F32)
        tb_ref[...] = jnp.zeros(tb_ref.shape, F32)
        tc_ref[...] = jnp.zeros(tc_ref.shape, F32)

    xs = _conv_silu(x_ref, tx_ref, ex_ref, cwx_ref, cbx_ref)
    bm = _conv_silu(b_ref, tb_ref, eb_ref, cwb_ref, cbb_ref)
    cm = _conv_silu(c_ref, tc_ref, ec_ref, cwc_ref, cbc_ref)

    dt = _softplus(dt_ref[...] + pr_ref[0:1])
    a = -jnp.exp(pr_ref[1:2])
    dta = dt * a

    ri = lax.broadcasted_iota(jnp.int32, (q, q), 0)
    ci = lax.broadcasted_iota(jnp.int32, (q, q), 1)
    causal = ci <= ri
    tri = causal.astype(BF16)
    hi, mid, lo = _split3(dta)
    cum = _dot(tri, hi) + _dot(tri, mid) + _dot(tri, lo)
    cum_t = cum.T

    er = lax.broadcasted_iota(jnp.int32, (LANES, gw), 0)
    el = lax.broadcasted_iota(jnp.int32, (LANES, gw), 1)
    expand = (el // SSM_HEAD_DIM == er).astype(BF16)
    dt_e = _dot_exact_rhs(dt, expand)
    cum_e = _dot_exact_rhs(cum, expand)
    exp_cum_e = jnp.exp(cum_e)
    cum_last_e = cum_e[q - 1:q]
    decay_end_e = jnp.exp(cum_last_e - cum_e)

    xdt = xs * dt_e
    xdt_b = xdt.astype(BF16)
    bm_b = bm.astype(BF16)
    cm_b = cm.astype(BF16)
    cb = _dot_nt(cm_b, bm_b)

    lane = lax.broadcasted_iota(jnp.int32, (q, LANES), 1)
    y_parts = []
    for pp in range(SSM_HPG // 2):
        xp = xdt_b[:, pp * LANES:(pp + 1) * LANES]
        ys = []
        for r in (2 * pp, 2 * pp + 1):
            seg = cum[:, r:r + 1] - cum_t[r:r + 1, :]
            lmat = jnp.exp(jnp.where(causal, seg, NEG_INF))
            ys.append(_dot((cb * lmat).astype(BF16), xp))
        y_parts.append(jnp.where(lane < SSM_HEAD_DIM, ys[0], ys[1]))
    y_diag = jnp.concatenate(y_parts, axis=1)

    h_prev = h_ref[...]
    y_off = _dot(cm_b, h_prev.astype(BF16)) * exp_cum_e
    states = _dot(bm.T.astype(BF16), (xdt * decay_end_e).astype(BF16))
    h_new = h_prev * jnp.exp(cum_last_e) + states
    h_ref[...] = h_new

    y = y_diag + y_off + dsk_ref[...] * xs
    y = y * _silu(z_ref[...])
    y_ref[...] = _rms(y) * nw_ref[...]

    @pl.when(c == pl.num_programs(1) - 1)
    def _():
        hout_ref[...] = h_new


def _ssd_prompt(zx, conv_w, conv_b, params, d_skip_e, norm_w):
    m = zx.shape[0]
    q = min(SSM_CHUNK, m)
    gw = SSM_GROUP_W
    xcol = SSM_INNER // gw
    bcol = 2 * SSM_INNER // SSM_STATE
    ccol = bcol + SSM_GROUPS
    dcol = ccol + SSM_GROUPS
    tail = SUBLANES
    cw2 = conv_w
    cb2 = conv_b.reshape(1, -1)
    in_specs = [
        pl.BlockSpec((q, gw), lambda g, c: (c, xcol + g)),
        pl.BlockSpec((q, SSM_STATE), lambda g, c: (c, bcol + g)),
        pl.BlockSpec((q, SSM_STATE), lambda g, c: (c, ccol + g)),
        pl.BlockSpec((q, gw), lambda g, c: (c, g)),
        pl.BlockSpec((q, LANES), lambda g, c: (c, dcol + g)),
        pl.BlockSpec((SSM_CONV, gw), lambda g, c: (0, g)),
        pl.BlockSpec((SSM_CONV, SSM_STATE), lambda g, c: (0, SSM_INNER // SSM_STATE + g)),
        pl.BlockSpec((SSM_CONV, SSM_STATE), lambda g, c: (0, SSM_INNER // SSM_STATE + SSM_GROUPS + g)),
        pl.BlockSpec((1, gw), lambda g, c: (0, g)),
        pl.BlockSpec((1, SSM_STATE), lambda g, c: (0, SSM_INNER // SSM_STATE + g)),
        pl.BlockSpec((1, SSM_STATE), lambda g, c: (0, SSM_INNER // SSM_STATE + SSM_GROUPS + g)),
        pl.BlockSpec((None, SUBLANES, LANES), lambda g, c: (g, 0, 0)),
        pl.BlockSpec((1, gw), lambda g, c: (0, g)),
        pl.BlockSpec((1, gw), lambda g, c: (0, g)),
    ]
    return pl.pallas_call(
        _ssd_p_kernel,
        grid=(SSM_GROUPS, m // q),
        in_specs=in_specs,
        out_specs=[pl.BlockSpec((q, gw), lambda g, c: (c, g)),
                   pl.BlockSpec((None, SSM_STATE, gw), lambda g, c: (g, 0, 0))],
        out_shape=[jax.ShapeDtypeStruct((m, SSM_INNER), F32),
                   jax.ShapeDtypeStruct((SSM_GROUPS, SSM_STATE, gw), F32)],
        scratch_shapes=[pltpu.VMEM((SSM_STATE, gw), F32),
                        pltpu.VMEM((tail, gw), F32), pltpu.VMEM((tail, SSM_STATE), F32),
                        pltpu.VMEM((tail, SSM_STATE), F32),
                        pltpu.VMEM((tail + q, gw), F32), pltpu.VMEM((tail + q, SSM_STATE), F32),
                        pltpu.VMEM((tail + q, SSM_STATE), F32)],
        compiler_params=_params(("arbitrary", "arbitrary")),
        name="ssd_prompt",
    )(zx, zx, zx, zx, zx, cw2, cw2, cw2, cb2, cb2, cb2, params, d_skip_e.reshape(1, -1), norm_w.reshape(1, -1))


def _ssd_s_pre_kernel(xbc_ref, hist_ref, cw_ref, cb_ref, dt_ref, dtb_ref, alog_ref, xs_ref, bc_ref, xdt_ref, dec_ref):
    out = cb_ref[...] + xbc_ref[...] * cw_ref[SSM_CONV - 1:SSM_CONV]
    for tau in range(SSM_CONV - 1):
        out = out + hist_ref[tau] * cw_ref[tau:tau + 1]
    act = _silu(out)
    xs = act[:, :SSM_INNER]
    dt = _softplus(dt_ref[...] + dtb_ref[...])
    xs_ref[...] = xs
    bc_ref[...] = act[:, SSM_INNER:]
    xdt_ref[...] = xs * dt
    dec_ref[...] = jnp.exp(dt * -jnp.exp(alog_ref[...]))


def _ssd_decode_pre(xbc, hist_t, conv_w, conv_b, dt_e, dtb_e, alog_e):
    b = xbc.shape[0]
    full = lambda a: pl.BlockSpec(a.shape, lambda i: (0,) * a.ndim)
    args = (xbc, hist_t, conv_w, conv_b.reshape(1, -1), dt_e, dtb_e.reshape(1, -1), alog_e.reshape(1, -1))
    shapes = [jax.ShapeDtypeStruct((b, SSM_INNER), F32), jax.ShapeDtypeStruct((b, 2 * SSM_BC_W), F32),
              jax.ShapeDtypeStruct((b, SSM_INNER), F32), jax.ShapeDtypeStruct((b, SSM_INNER), F32)]
    return pl.pallas_call(
        _ssd_s_pre_kernel,
        grid=(1,),
        in_specs=[full(a) for a in args],
        out_specs=[pl.BlockSpec(s.shape, lambda i: (0, 0)) for s in shapes],
        out_shape=shapes,
        compiler_params=_params(("arbitrary",)),
        name="ssd_decode_pre",
    )(*args)


def _ssd_s_state_kernel(h_ref, xdt_ref, dec_ref, b_ref, c_ref, ho_ref, y_ref):
    nblk = xdt_ref.shape[1]
    rows_per_group = SSM_GROUP_W
    lane = lax.broadcasted_iota(jnp.int32, y_ref.shape, 1)
    y = jnp.zeros(y_ref.shape, F32)
    for blk in range(nblk):
        g = (blk * LANES) // rows_per_group
        rs = slice(blk * LANES, (blk + 1) * LANES)
        h_new = (h_ref[rs, :] * dec_ref[:, blk:blk + 1]
                 + xdt_ref[:, blk:blk + 1] * b_ref[:, g * SSM_STATE:(g + 1) * SSM_STATE])
        ho_ref[rs, :] = h_new
        col = jnp.sum(h_new * c_ref[:, g * SSM_STATE:(g + 1) * SSM_STATE], axis=1, keepdims=True)
        y = jnp.where(lane == blk, col, y)
    y_ref[...] = y


def _ssd_decode_state(h0, xdt_t, dec_t, bm, cm):
    b, hp, n = h0.shape
    nblk = hp // LANES
    return pl.pallas_call(
        _ssd_s_state_kernel,
        grid=(b,),
        in_specs=[
            pl.BlockSpec((None, hp, n), lambda i: (i, 0, 0)),
            pl.BlockSpec((None, LANES, nblk), lambda i: (i, 0, 0)),
            pl.BlockSpec((None, LANES, nblk), lambda i: (i, 0, 0)),
            pl.BlockSpec((None, 1, SSM_BC_W), lambda i: (i, 0, 0)),
            pl.BlockSpec((None, 1, SSM_BC_W), lambda i: (i, 0, 0)),
        ],
        out_specs=[pl.BlockSpec((None, hp, n), lambda i: (i, 0, 0)),
                   pl.BlockSpec((None, LANES, nblk), lambda i: (i, 0, 0))],
        out_shape=[jax.ShapeDtypeStruct((b, hp, n), F32), jax.ShapeDtypeStruct((b, LANES, nblk), F32)],
        compiler_params=_params(("arbitrary",)),
        name="ssd_decode_state",
    )(h0, xdt_t, dec_t, bm, cm)


def _ssd_s_post_kernel(y_ref, xs_ref, z_ref, dsk_ref, nw_ref, o_ref):
    y = (y_ref[...] + dsk_ref[...] * xs_ref[...]) * _silu(z_ref[...])
    for g in range(SSM_GROUPS):
        cs = slice(g * SSM_GROUP_W, (g + 1) * SSM_GROUP_W)
        o_ref[:, cs] = _rms(y[:, cs]) * nw_ref[:, cs]


def _ssd_decode_post(y, xs, z, d_skip_e, norm_w):
    args = (y, xs, z, d_skip_e.reshape(1, -1), norm_w.reshape(1, -1))
    return pl.pallas_call(
        _ssd_s_post_kernel,
        grid=(1,),
        in_specs=[pl.BlockSpec(a.shape, lambda i: (0, 0)) for a in args],
        out_specs=pl.BlockSpec(y.shape, lambda i: (0, 0)),
        out_shape=jax.ShapeDtypeStruct(y.shape, F32),
        compiler_params=_params(("arbitrary",)),
        name="ssd_decode_post",
    )(*args)


def _route_rows(x, sc, sh, wr_ref, br_ref, hn_ref, ti_ref, tp_ref):
    rows = x.shape[0]
    hn = _rms(x) * (1.0 + sc) + sh
    hn_ref[0:rows] = hn
    ah, am, al = _split3(hn)
    wh, wm, wl = _split3(wr_ref[...])
    lg = (_dot(ah, wh) + _dot(ah, wm) + _dot(am, wh) + _dot(ah, wl) + _dot(al, wh) + _dot(am, wm)) + br_ref[...]
    lane = lax.broadcasted_iota(jnp.int32, lg.shape, 1)
    lg = jnp.where(lane < N_EXPERTS, lg, NEG_INF)
    vals, idxs = [], []
    for _ in range(TOP_K):
        mx = jnp.max(lg, axis=1, keepdims=True)
        idx = jnp.min(jnp.where(lg == mx, lane, LANES), axis=1, keepdims=True)
        vals.append(mx)
        idxs.append(idx)
        lg = jnp.where(lane == idx, NEG_INF, lg)
    es = [jnp.exp(v - vals[0]) for v in vals]
    tot = es[0] + es[1] + es[2] + es[3]
    ti = jnp.zeros(lg.shape, jnp.int32)
    tp = jnp.zeros(lg.shape, F32)
    for k in range(TOP_K):
        ti = jnp.where(lane == k, idxs[k], ti)
        tp = jnp.where(lane == k, es[k] / tot, tp)
    ti_ref[0:rows] = ti
    tp_ref[0:rows] = tp


def _router_kernel(xp_ref, scp_ref, shp_ref, xs_ref, scs_ref, shs_ref, wr_ref, br_ref, hn_ref, ti_ref, tp_ref):
    last = pl.num_programs(0) - 1

    @pl.when(pl.program_id(0) < last)
    def _():
        _route_rows(xp_ref[...], scp_ref[...], shp_ref[...], wr_ref, br_ref, hn_ref, ti_ref, tp_ref)

    @pl.when(pl.program_id(0) == last)
    def _():
        _route_rows(xs_ref[...], scs_ref[...], shs_ref[...], wr_ref, br_ref, hn_ref, ti_ref, tp_ref)


def _norm_router(xp, sc_p, sh_p, xs, sc_s, sh_s, wr_pad, br_pad):
    mp, d = xp.shape
    nb = xs.shape[0]
    tm = min(TM, mp)
    assert mp % tm == 0 and nb <= tm
    np_ = mp // tm
    t_all = mp + nb
    row = pl.BlockSpec((1, d), lambda i: (0, 0))
    dec = pl.BlockSpec((nb, d), lambda i: (0, 0))
    return pl.pallas_call(
        _router_kernel,
        grid=(np_ + 1,),
        in_specs=[pl.BlockSpec((tm, d), lambda i: (jnp.minimum(i, np_ - 1), 0)), row, row, dec, dec, dec,
                  pl.BlockSpec((d, LANES), lambda i: (0, 0)), pl.BlockSpec((1, LANES), lambda i: (0, 0))],
        out_specs=[pl.BlockSpec((tm, d), lambda i: (i, 0)),
                   pl.BlockSpec((tm, LANES), lambda i: (i, 0)),
                   pl.BlockSpec((tm, LANES), lambda i: (i, 0))],
        out_shape=[jax.ShapeDtypeStruct((t_all, d), F32),
                   jax.ShapeDtypeStruct((t_all, LANES), jnp.int32),
                   jax.ShapeDtypeStruct((t_all, LANES), F32)],
        compiler_params=_params(("arbitrary",)),
        name="norm_router",
    )(xp, sc_p, sh_p, xs, sc_s, sh_s, wr_pad, br_pad)


def _route_plan(top_i, tm, n_tiles):
    t = top_i.shape[0]
    e_flat = top_i.reshape(-1)
    onehot = (e_flat[:, None] == jnp.arange(N_EXPERTS, dtype=jnp.int32)[None, :]).astype(jnp.int32)
    csum = jnp.cumsum(onehot, axis=0)
    rank = jnp.take_along_axis(csum, e_flat[:, None], axis=1)[:, 0] - 1
    cnt = csum[-1]
    tiles_e = (cnt + tm - 1) // tm
    tiles_end = jnp.cumsum(tiles_e)
    tile_start = tiles_end - tiles_e
    dest = (tile_start[e_flat] * tm + rank).astype(jnp.int32)
    total = tiles_end[-1]
    tile_ids = jnp.arange(n_tiles, dtype=jnp.int32)
    tile_valid = (tile_ids < total).astype(jnp.int32)
    tile_expert = jnp.sum((tiles_end[None, :] <= tile_ids[:, None]).astype(jnp.int32), axis=1)
    tile_expert = jnp.minimum(tile_expert, N_EXPERTS - 1)
    last_expert = tile_expert[jnp.maximum(total - 1, 0)]
    tile_expert = jnp.where(tile_valid == 1, tile_expert, last_expert)
    tile_src = jnp.minimum(tile_ids, jnp.maximum(total - 1, 0)).astype(jnp.int32)
    return dest, tile_expert.astype(jnp.int32), tile_valid, tile_src


def _dispatch_kernel(dest_ref, hn_ref, xs_in, xs_hbm, sem, *, tc):
    del xs_in
    base = pl.program_id(0) * tc

    def body(j, carry):
        for k in range(TOP_K):
            row = dest_ref[(base + j) * TOP_K + k]
            pltpu.make_async_copy(hn_ref.at[pl.ds(j, 1)], xs_hbm.at[pl.ds(row, 1)], sem).start()
        return carry

    lax.fori_loop(0, tc, body, 0)
    for _ in range(TOP_K):
        pltpu.make_async_copy(hn_ref, xs_hbm.at[pl.ds(0, tc)], sem).wait()


def _dispatch_rows(t):
    best = None
    for cand in range(SUBLANES, min(t, DISPATCH_MAX_ROWS) + 1, SUBLANES):
        if t % cand == 0:
            best = cand
    assert best is not None, "token count must be a multiple of the sublane count"
    return best


def _moe_dispatch(dest, hn_all, n_rows):
    t, w = hn_all.shape
    tc = _dispatch_rows(t)
    grid_spec = pltpu.PrefetchScalarGridSpec(
        num_scalar_prefetch=1,
        grid=(t // tc,),
        in_specs=[pl.BlockSpec((tc, w), lambda i, dr: (i, 0)), pl.BlockSpec(memory_space=pl.ANY)],
        out_specs=pl.BlockSpec(memory_space=pl.ANY),
        scratch_shapes=[pltpu.SemaphoreType.DMA(())],
    )
    return pl.pallas_call(
        functools.partial(_dispatch_kernel, tc=tc),
        grid_spec=grid_spec,
        out_shape=jax.ShapeDtypeStruct((n_rows, w), hn_all.dtype),
        input_output_aliases={2: 0},
        compiler_params=_params(("arbitrary",)),
        name="moe_dispatch",
    )(dest, hn_all, jnp.zeros((n_rows, w), hn_all.dtype))


def _moe_kernel(te_ref, tv_ref, ts_ref, xs_ref, sel_ref, wgu_ref, bgu_ref, wdn_ref, bdn_ref, o_ref, xb_ref, *, tn):
    r = pl.program_id(0)
    c = pl.program_id(1)
    del te_ref, ts_ref

    @pl.when((c == 0) & (tv_ref[r] == 0))
    def _():
        o_ref[...] = jnp.zeros(o_ref.shape, F32)

    @pl.when((c == 0) & (tv_ref[r] == 1))
    def _():
        xb_ref[...] = xs_ref[...].astype(BF16)
        o_ref[...] = jnp.broadcast_to(bdn_ref[...], o_ref.shape)

    @pl.when(tv_ref[r] == 1)
    def _():
        sw = sel_ref.shape[0]
        gu = _dot(xb_ref[...], wgu_ref[...].astype(BF16)) + bgu_ref[...]
        up = jnp.concatenate([pltpu.roll(gu[:, j * LANES:(j + 1) * LANES], LANES - 1, 1)
                              for j in range(tn // LANES)], axis=1)
        gate = jnp.minimum(gu, SWIGLU_LIMIT)
        up = jnp.clip(up, -SWIGLU_LIMIT, SWIGLU_LIMIT)
        act = ((up + 1.0) * gate * jax.nn.sigmoid(SWIGLU_ALPHA * gate)).astype(BF16)
        parts = [_dot(act[:, k * sw:(k + 1) * sw], sel_ref[...]).astype(BF16) for k in range(tn // sw)]
        o_ref[...] += _dot(jnp.concatenate(parts, axis=1), wdn_ref[...].astype(BF16))


def _moe_tm(n_tokens):
    fair = -(-(n_tokens * TOP_K) // N_EXPERTS)
    return 16 * (-(-(fair * 14) // (25 * 16)))


def _moe_experts(xs, plan, w_gu, b_gu, w_dn, b_dn, layer, tm, n_tiles):
    _, tile_expert, tile_valid, tile_src = plan
    d = w_dn.shape[-1]
    f2 = w_gu.shape[-1]
    tn = MOE_TN
    ne = w_gu.shape[1]
    sw = MOE_SEL_W
    sel = (jnp.arange(sw)[:, None] == 2 * jnp.arange(sw // 2)[None, :]).astype(BF16)
    grid_spec = pltpu.PrefetchScalarGridSpec(
        num_scalar_prefetch=3,
        grid=(n_tiles, f2 // tn),
        in_specs=[
            pl.BlockSpec((tm, d), lambda r, c, te, tv, ts: (ts[r], 0)),
            pl.BlockSpec((sw, sw // 2), lambda r, c, te, tv, ts: (0, 0)),
            pl.BlockSpec((None, None, d, tn), lambda r, c, te, tv, ts: (layer, te[r], 0, c)),
            pl.BlockSpec((None, None, 1, tn), lambda r, c, te, tv, ts: (layer, te[r], 0, c)),
            pl.BlockSpec((None, None, tn // 2, d), lambda r, c, te, tv, ts: (layer, te[r], c, 0)),
            pl.BlockSpec((None, None, 1, d), lambda r, c, te, tv, ts: (layer, te[r], 0, 0)),
        ],
        out_specs=pl.BlockSpec((tm, d), lambda r, c, te, tv, ts: (r, 0)),
        scratch_shapes=[pltpu.VMEM((tm, d), BF16)],
    )
    return pl.pallas_call(
        functools.partial(_moe_kernel, tn=tn),
        grid_spec=grid_spec,
        out_shape=jax.ShapeDtypeStruct((n_tiles * tm, d), F32),
        compiler_params=_params(("arbitrary", "arbitrary")),
        name="moe_experts",
    )(tile_expert, tile_valid, tile_src, xs, sel, w_gu,
      b_gu.reshape(b_gu.shape[0], ne, 1, f2), w_dn, b_dn.reshape(b_dn.shape[0], ne, 1, d))


def _combine_kernel(dest_ref, ys_hbm, tp_ref, x_ref, g_ref, fw_ref, o_ref, yb_ref, sem, *, tc, tok0, final):
    i = pl.program_id(0)
    nsteps = pl.num_programs(0)

    def start_gather(tile, s):
        base = (tok0 + tile * tc) * TOP_K

        def body(j, carry):
            for k in range(TOP_K):
                row = dest_ref[base + j * TOP_K + k]
                pltpu.make_async_copy(ys_hbm.at[pl.ds(row, 1)], yb_ref.at[s, k, pl.ds(j, 1)], sem.at[s]).start()
            return carry

        lax.fori_loop(0, tc, body, 0)

    slot = i % 2

    @pl.when(i == 0)
    def _():
        start_gather(0, 0)

    for k in range(TOP_K):
        pltpu.make_async_copy(ys_hbm.at[pl.ds(0, tc)], yb_ref.at[slot, k], sem.at[slot]).wait()

    @pl.when(i + 1 < nsteps)
    def _():
        start_gather(i + 1, 1 - slot)

    tp = tp_ref[...]
    acc = yb_ref[slot, 0] * tp[:, 0:1]
    for k in range(1, TOP_K):
        acc = acc + yb_ref[slot, k] * tp[:, k:k + 1]
    out = x_ref[...] + g_ref[...] * acc
    if final:
        out = _rms(out) * fw_ref[...]
    o_ref[...] = out


def _moe_combine(dest, ys, top_p, x, gate, final_w, tok0, final):
    m, d = x.shape
    tc = min(256, m)
    mg = gate.shape[0]
    blk0 = tok0 // tc
    grid_spec = pltpu.PrefetchScalarGridSpec(
        num_scalar_prefetch=1,
        grid=(m // tc,),
        in_specs=[
            pl.BlockSpec(memory_space=pl.ANY),
            pl.BlockSpec((tc, LANES), lambda i, dr: (blk0 + i, 0)),
            pl.BlockSpec((tc, d), lambda i, dr: (i, 0)),
            pl.BlockSpec((tc if mg > 1 else 1, d), (lambda i, dr: (i, 0)) if mg > 1 else (lambda i, dr: (0, 0))),
            pl.BlockSpec((1, d), lambda i, dr: (0, 0)),
        ],
        out_specs=pl.BlockSpec((tc, d), lambda i, dr: (i, 0)),
        scratch_shapes=[pltpu.VMEM((2, TOP_K, tc, d), F32), pltpu.SemaphoreType.DMA((2,))],
    )
    return pl.pallas_call(
        functools.partial(_combine_kernel, tc=tc, tok0=tok0, final=final),
        grid_spec=grid_spec,
        out_shape=jax.ShapeDtypeStruct((m, d), F32),
        compiler_params=_params(("arbitrary",)),
        name="moe_combine",
    )(dest, ys, top_p, x, gate, final_w.reshape(1, d))


def _rope_tables(pos):
    half = A_QK_DIM // 2
    inv = ROPE_THETA ** (-jnp.arange(half, dtype=F32) * 2.0 / A_QK_DIM)
    ang = pos.astype(F32)[:, None] * inv[None, :]
    cos, sin = jnp.cos(ang), jnp.sin(ang)
    cos_t = jnp.tile(cos, (1, LANES // half))
    sin_t = jnp.tile(jnp.concatenate([-sin, sin], axis=1), (1, LANES // A_QK_DIM))
    return cos_t, sin_t


def kernel(x_prompt, x_sample, cache_k, cache_v, state_pool, state_conv, state_ssm, page_table, c_prompt, c_sample,
           ada_w, ada_b, ab_w_in, ab_lambda, ab_subln, ab_pool_w, ab_pool_scale, ab_w_out,
           ssd_w_in, ssd_conv_w, ssd_conv_b, ssd_dt_bias, ssd_a_log, ssd_d, ssd_norm, ssd_w_out,
           router_w, router_b, moe_w_gu, moe_b_gu, moe_w_dn, moe_b_dn, final_norm):
    d = D_MODEL
    seq = x_prompt.shape[1]
    nb = x_sample.shape[0]
    n_pages = page_table.shape[1]
    past_len = n_pages * cache_k.shape[2]
    t_all = seq + nb
    moe_tm = _moe_tm(t_all)
    n_tiles = -(-(t_all * TOP_K) // moe_tm) + N_EXPERTS

    c_rows = SUBLANES * (-(-(1 + nb) // SUBLANES))
    c_all = jnp.concatenate([c_prompt, c_sample, jnp.zeros((c_rows - 1 - nb, d), F32)], axis=0)
    mod = _ada_mod(c_all, ada_w, ada_b).reshape(DEPTH, c_rows, N_MOD, d)

    def mods(layer, m):
        return mod[layer, 0:1, m], mod[layer, 1:1 + nb, m]

    xp = x_prompt.reshape(seq, d)
    xs_ = x_sample.reshape(nb, d)
    rope_p = _rope_tables(jnp.arange(seq))
    rope_s = _rope_tables(jnp.full((nb,), past_len))
    nl_c, n_phys, page = cache_k.shape[:3]
    cache_k4 = jnp.transpose(cache_k, (0, 1, 3, 4, 5, 2)).reshape(nl_c, n_phys, QK_WIDTH, page)
    cache_v4 = cache_v.reshape(nl_c, n_phys, page * A_HEADS, A_V_DIM)

    def moe_block(layer, xp, xs_, final):
        sh_p, sh_s = mods(layer, 3)
        sc_p, sc_s = mods(layer, 4)
        g_p, g_s = mods(layer, 5)
        wr = jnp.pad(router_w[layer], ((0, 0), (0, LANES - N_EXPERTS)))
        br = jnp.pad(router_b[layer], (0, LANES - N_EXPERTS)).reshape(1, LANES)
        hn_all, ti_all, tp_all = _norm_router(xp, sc_p, sh_p, xs_, sc_s, sh_s, wr, br)
        plan = _route_plan(ti_all[:, :TOP_K], moe_tm, n_tiles)
        xs_sorted = _moe_dispatch(plan[0], hn_all, n_tiles * moe_tm)
        ys = _moe_experts(xs_sorted, plan, moe_w_gu, moe_b_gu, moe_w_dn, moe_b_dn, layer, moe_tm, n_tiles)
        xp = _moe_combine(plan[0], ys, tp_all, xp, g_p, final_norm, 0, final)
        xs_ = _moe_combine(plan[0], ys, tp_all, xs_, g_s, final_norm, seq, final)
        return xp, xs_

    sh_p, sh_s = mods(0, 0)
    sc_p, sc_s = mods(0, 1)
    g_p, g_s = mods(0, 2)
    lam0 = _lambda_init(0)
    w_in0 = ab_w_in[0].astype(BF16)
    w_out0 = ab_w_out[0].astype(BF16)
    ucol = 2 * QK_WIDTH + A_WIDTH

    qkvu_p = _norm_matmul(xp, sc_p, sh_p, w_in0, 1024, rope_p, 2 * QK_WIDTH)
    o_p = _attn_prompt(qkvu_p, ab_lambda[0], ab_subln[0], lam0)
    pg_p = _pool_prompt(qkvu_p, ab_pool_w[0], ab_pool_scale[0])
    k_prompt = qkvu_p[:, QK_WIDTH:2 * QK_WIDTH].reshape(1, 1, seq, A_HEADS, 2, A_QK_DIM)
    v_prompt = qkvu_p[:, 2 * QK_WIDTH:ucol].reshape(1, 1, seq, A_HEADS, A_V_DIM)
    pool_prompt = qkvu_p[seq - POOL_HIST:, ucol:].reshape(1, 1, POOL_HIST, POOL_WIDTH)
    xp = _matmul_resid([o_p, pg_p], w_out0, xp, g_p)

    qkvu_s = _norm_matmul(xs_, sc_s, sh_s, w_in0, 1024, rope_s, 2 * QK_WIDTH)
    q_s = qkvu_s[:, :QK_WIDTH].reshape(nb, 1, QK_WIDTH)
    k_s = qkvu_s[:, QK_WIDTH:2 * QK_WIDTH].reshape(nb, 1, QK_WIDTH)
    v_s = qkvu_s[:, 2 * QK_WIDTH:ucol].reshape(nb, 1, A_WIDTH)
    u_s = qkvu_s[:, ucol:]
    o_s = _attn_decode(q_s, k_s, v_s, cache_k4, cache_v4, 0, page_table, ab_lambda[0], ab_subln[0], lam0)
    pg_s = _pool_decode(u_s, jnp.swapaxes(state_pool[0], 0, 1), ab_pool_w[0], ab_pool_scale[0])
    k_sample = k_s.reshape(1, nb, 1, A_HEADS, 2, A_QK_DIM)
    v_sample = v_s.reshape(1, nb, 1, A_HEADS, A_V_DIM)
    pool_sample = jnp.concatenate([state_pool[0][:, 1:], u_s[:, None, :]], axis=1)[None]
    xs_ = _matmul_resid([o_s.reshape(nb, A_WIDTH), pg_s], w_out0, xs_, g_s)

    xp, xs_ = moe_block(0, xp, xs_, False)

    sh_p, sh_s = mods(1, 0)
    sc_p, sc_s = mods(1, 1)
    g_p, g_s = mods(1, 2)
    zx_w = SSM_INNER + SSM_CONV_DIM
    w_dt = ssd_w_in[0][:, zx_w:].reshape(d, SSM_GROUPS, SSM_HPG)
    w_dt = jnp.pad(w_dt, ((0, 0), (0, 0), (0, LANES - SSM_HPG))).reshape(d, SSM_GROUPS * LANES)
    w_in1 = jnp.concatenate([ssd_w_in[0][:, :zx_w], w_dt], axis=1).astype(BF16)
    w_out1 = ssd_w_out[0].astype(BF16)
    tn1 = w_in1.shape[1] // SSM_GROUPS

    def per_group(v):
        return jnp.pad(v.reshape(SSM_GROUPS, SSM_HPG), ((0, 0), (0, LANES - SSM_HPG)))

    zeros_g = jnp.zeros((SSM_GROUPS, LANES), F32)
    params = jnp.stack([per_group(ssd_dt_bias[0]), per_group(ssd_a_log[0])] + [zeros_g] * (SUBLANES - 2), axis=1)
    dsk_e = jnp.repeat(ssd_d[0], SSM_HEAD_DIM)
    dtb_e = jnp.repeat(ssd_dt_bias[0], SSM_HEAD_DIM)
    alog_e = jnp.repeat(ssd_a_log[0], SSM_HEAD_DIM)

    zx_p = _norm_matmul(xp, sc_p, sh_p, w_in1, tn1)
    y_p, h_p = _ssd_prompt(zx_p, ssd_conv_w[0], ssd_conv_b[0], params, dsk_e, ssd_norm[0])
    conv_prompt = zx_p[seq - (SSM_CONV - 1):, SSM_INNER:zx_w].reshape(1, 1, SSM_CONV - 1, SSM_CONV_DIM)
    ssm_prompt = jnp.transpose(h_p.reshape(SSM_GROUPS, SSM_STATE, SSM_HPG, SSM_HEAD_DIM), (0, 2, 3, 1))
    ssm_prompt = ssm_prompt.reshape(1, 1, SSM_HEADS, SSM_HEAD_DIM, SSM_STATE)
    xp = _matmul_resid([y_p], w_out1, xp, g_p)

    zx_s = _norm_matmul(xs_, sc_s, sh_s, w_in1, tn1)
    z_s = zx_s[:, :SSM_INNER]
    xbc_s = zx_s[:, SSM_INNER:zx_w]
    dt_s = zx_s[:, zx_w:].reshape(nb, SSM_GROUPS, LANES)[:, :, :SSM_HPG].reshape(nb, SSM_HEADS)
    dt_e = jnp.repeat(dt_s, SSM_HEAD_DIM, axis=1)
    xs_act, bc_s, xdt_s, dec_s = _ssd_decode_pre(xbc_s, jnp.swapaxes(state_conv[0], 0, 1), ssd_conv_w[0],
                                                 ssd_conv_b[0], dt_e, dtb_e, alog_e)
    nblk = SSM_INNER // LANES

    def col_major(v):
        return jnp.swapaxes(v.reshape(nb, nblk, LANES), 1, 2)

    h_s, y_t = _ssd_decode_state(state_ssm[0].reshape(nb, SSM_INNER, SSM_STATE), col_major(xdt_s), col_major(dec_s),
                                 bc_s[:, :SSM_BC_W].reshape(nb, 1, SSM_BC_W), bc_s[:, SSM_BC_W:].reshape(nb, 1, SSM_BC_W))
    y_s = jnp.swapaxes(y_t, 1, 2).reshape(nb, SSM_INNER)
    yn_s = _ssd_decode_post(y_s, xs_act, z_s, dsk_e, ssd_norm[0])
    conv_sample = jnp.concatenate([state_conv[0][:, 1:], xbc_s[:, None, :]], axis=1)[None]
    ssm_sample = h_s.reshape(1, nb, SSM_HEADS, SSM_HEAD_DIM, SSM_STATE)
    xs_ = _matmul_resid([yn_s], w_out1, xs_, g_s)

    xp, xs_ = moe_block(1, xp, xs_, True)

    return (xp.reshape(1, seq, d), xs_.reshape(nb, 1, d), k_prompt, v_prompt, pool_prompt, conv_prompt, ssm_prompt,
            k_sample, v_sample, pool_sample, conv_sample, ssm_sample)
```

```python
import functools
import math

import jax
import jax.numpy as jnp
from jax import lax
from jax.experimental import pallas as pl
from jax.experimental.pallas import tpu as pltpu

F32 = jnp.float32
BF16 = jnp.bfloat16
NEG_INF = float("-inf")
LOG2_E = math.log2(math.e)

D_MODEL = 2048
DEPTH = 2
PAGE_SIZE = 128
NORM_EPS = 1e-5
N_MOD = 6
A_HEADS = 8
A_QK_DIM = 64
A_V_DIM = 128
QK_WIDTH = A_HEADS * 2 * A_QK_DIM
A_WIDTH = A_HEADS * A_V_DIM
A_SCALE = A_QK_DIM ** -0.5
ROPE_THETA = 10000.0
POOL_WINDOWS = (2, 4, 8, 16)
POOL_WIDTH = D_MODEL // 2
POOL_GROUP_DIM = POOL_WIDTH // 4
POOL_HIST = 15
SSM_INNER = 2 * D_MODEL
SSM_HEAD_DIM = 64
SSM_HEADS = SSM_INNER // SSM_HEAD_DIM
SSM_GROUPS = 8
SSM_HPG = SSM_HEADS // SSM_GROUPS
SSM_STATE = 128
SSM_CONV = 4
SSM_CHUNK = 128
SSM_GROUP_W = SSM_INNER // SSM_GROUPS
SSM_BC_W = SSM_GROUPS * SSM_STATE
SSM_CONV_DIM = SSM_INNER + 2 * SSM_BC_W
N_EXPERTS = 32
TOP_K = 4
SWIGLU_ALPHA = 1.702
SWIGLU_LIMIT = 7.0

LANES = 128
SUBLANES = 8
VMEM_LIMIT = 56 * 1024 * 1024

TM = 512
NORM_MM_TM = 1024
DISPATCH_MAX_ROWS = 2056
MOE_TN = 1024
MOE_SEL_W = 512
ATT_TQ = 1024
ATT_TK = 512
DEC_PPS = 16


def _params(sem):
    return pltpu.CompilerParams(dimension_semantics=sem, vmem_limit_bytes=VMEM_LIMIT)


def _lambda_init(layer):
    return 0.8 - 0.6 * math.exp(-0.3 * layer)


def _split3(x):
    hi = x.astype(BF16)
    r1 = x - hi.astype(F32)
    mid = r1.astype(BF16)
    lo = (r1 - mid.astype(F32)).astype(BF16)
    return hi, mid, lo


def _dot(a, b):
    return jnp.dot(a, b, preferred_element_type=F32)


def _dot_nt(a, b):
    return lax.dot_general(a, b, (((1,), (1,)), ((), ())), preferred_element_type=F32)


def _dot_exact_rhs(x, sel):
    hi, mid, lo = _split3(x)
    return _dot(hi, sel) + _dot(mid, sel) + _dot(lo, sel)


def _silu(x):
    return x * jax.nn.sigmoid(x)


def _rms(x):
    return x * lax.rsqrt(jnp.mean(x * x, axis=-1, keepdims=True) + NORM_EPS)


def _ada_kernel(c_ref, w_ref, b_ref, o_ref):
    a = _silu(c_ref[...]).astype(BF16)
    o_ref[...] = _dot(a, w_ref[...].astype(BF16)) + b_ref[...]


def _ada_mod(c_all, ada_w, ada_b, tn=1024):
    m, d = c_all.shape
    nl, _, n = ada_w.shape
    return pl.pallas_call(
        _ada_kernel,
        grid=(nl, n // tn),
        in_specs=[
            pl.BlockSpec((m, d), lambda l, j: (0, 0)),
            pl.BlockSpec((None, d, tn), lambda l, j: (l, 0, j)),
            pl.BlockSpec((None, 1, tn), lambda l, j: (l, 0, j)),
        ],
        out_specs=pl.BlockSpec((None, m, tn), lambda l, j: (l, 0, j)),
        out_shape=jax.ShapeDtypeStruct((nl, m, n), F32),
        compiler_params=_params(("arbitrary", "arbitrary")),
        name="ada_mod",
    )(c_all, ada_w, ada_b.reshape(nl, 1, n))


def _norm_mm_kernel(*refs, rope_tiles, tn):
    if rope_tiles:
        x_ref, sc_ref, sh_ref, w_ref, cos_ref, sin_ref, o_ref, hn_ref = refs
    else:
        x_ref, sc_ref, sh_ref, w_ref, o_ref, hn_ref = refs
    j = pl.program_id(1)

    @pl.when(j == 0)
    def _():
        hn_ref[...] = (_rms(x_ref[...]) * (1.0 + sc_ref[...]) + sh_ref[...]).astype(BF16)

    y = _dot(hn_ref[...], w_ref[...])
    if not rope_tiles:
        o_ref[...] = y
        return

    @pl.when(j >= rope_tiles)
    def _():
        o_ref[...] = y

    @pl.when(j < rope_tiles)
    def _():
        cos = cos_ref[...]
        sin = sin_ref[...]
        lane = lax.broadcasted_iota(jnp.int32, cos.shape, 1)
        first = (lane % A_QK_DIM) < (A_QK_DIM // 2)
        for c in range(tn // LANES):
            yc = y[:, c * LANES:(c + 1) * LANES]
            partner = jnp.where(first, pltpu.roll(yc, LANES - A_QK_DIM // 2, 1),
                                pltpu.roll(yc, A_QK_DIM // 2, 1))
            o_ref[:, c * LANES:(c + 1) * LANES] = yc * cos + partner * sin


def _norm_matmul(x, scale, shift, w, tn, rope=None, rope_cols=0):
    m, d = x.shape
    n = w.shape[1]
    tm = min(NORM_MM_TM, m)
    ms = scale.shape[0]
    mod_spec = pl.BlockSpec((tm if ms > 1 else 1, d), (lambda i, j: (i, 0)) if ms > 1 else (lambda i, j: (0, 0)))
    in_specs = [pl.BlockSpec((tm, d), lambda i, j: (i, 0)), mod_spec, mod_spec,
                pl.BlockSpec((d, tn), lambda i, j: (0, j))]
    args = [x, scale, shift, w]
    if rope is not None:
        in_specs += [pl.BlockSpec((tm, LANES), lambda i, j: (i, 0))] * 2
        args += list(rope)
    return pl.pallas_call(
        functools.partial(_norm_mm_kernel, rope_tiles=rope_cols // tn, tn=tn),
        grid=(m // tm, n // tn),
        in_specs=in_specs,
        out_specs=pl.BlockSpec((tm, tn), lambda i, j: (i, j)),
        out_shape=jax.ShapeDtypeStruct((m, n), F32),
        scratch_shapes=[pltpu.VMEM((tm, d), BF16)],
        compiler_params=_params(("arbitrary", "arbitrary")),
        name="norm_matmul",
    )(*args)


def _mm_resid_kernel(*refs, n_a):
    a_refs = refs[:n_a]
    w_ref, x_ref, g_ref, o_ref, ab_ref = refs[n_a:]

    @pl.when(pl.program_id(1) == 0)
    def _():
        off = 0
        for a_ref in a_refs:
            ka = a_ref.shape[1]
            ab_ref[:, off:off + ka] = a_ref[...].astype(BF16)
            off += ka

    o_ref[...] = x_ref[...] + g_ref[...] * _dot(ab_ref[...], w_ref[...])


def _matmul_resid(a_list, w, x, gate, tn=1024):
    m, n = x.shape
    k = w.shape[0]
    tm = min(TM, m)
    mg = gate.shape[0]
    in_specs = [pl.BlockSpec((tm, a.shape[1]), lambda i, j: (i, 0)) for a in a_list]
    in_specs += [
        pl.BlockSpec((k, tn), lambda i, j: (0, j)),
        pl.BlockSpec((tm, tn), lambda i, j: (i, j)),
        pl.BlockSpec((tm if mg > 1 else 1, tn), (lambda i, j: (i, j)) if mg > 1 else (lambda i, j: (0, j))),
    ]
    return pl.pallas_call(
        functools.partial(_mm_resid_kernel, n_a=len(a_list)),
        grid=(m // tm, n // tn),
        in_specs=in_specs,
        out_specs=pl.BlockSpec((tm, tn), lambda i, j: (i, j)),
        out_shape=jax.ShapeDtypeStruct((m, n), F32),
        scratch_shapes=[pltpu.VMEM((tm, k), BF16)],
        compiler_params=_params(("arbitrary", "arbitrary")),
        name="matmul_resid",
    )(*a_list, w, x, gate)


def _lam_value(lam_ref, lam_init):
    lv = lam_ref[...]
    s01 = jnp.sum(lv[0:1] * lv[1:2], axis=1, keepdims=True)
    s23 = jnp.sum(lv[2:3] * lv[3:4], axis=1, keepdims=True)
    return jnp.exp(s01) - jnp.exp(s23) + lam_init


def _attn_p_kernel(qi_ref, ki_ref, q_ref, k_ref, v_ref, lam_ref, sub_ref, o_ref, m_ref, l_ref, acc_ref, *, lam_init):
    p = pl.program_id(1)
    qi = qi_ref[p]
    ki = ki_ref[p]
    tq = q_ref.shape[0]
    tk = k_ref.shape[0]
    ratio = tq // tk
    shift = ki * tk - qi * tq

    @pl.when(ki == 0)
    def _():
        m_ref[...] = jnp.full(m_ref.shape, NEG_INF, F32)
        l_ref[...] = jnp.zeros(l_ref.shape, F32)
        acc_ref[...] = jnp.zeros(acc_ref.shape, F32)

    def step(masked):
        q = q_ref[...] * (A_SCALE * LOG2_E)
        lane = lax.broadcasted_iota(jnp.int32, q.shape, 1)
        kb = k_ref[...].astype(BF16)
        vb = v_ref[...].astype(BF16)
        if masked:
            row = lax.broadcasted_iota(jnp.int32, (tq, LANES), 0)
            col = lax.broadcasted_iota(jnp.int32, (tq, LANES), 1) + shift
        for c in range(2):
            qc = jnp.where((lane < A_QK_DIM) == (c == 0), q, 0.0).astype(BF16)
            s = _dot_nt(qc, kb)
            chunks = [s[:, j * LANES:(j + 1) * LANES] for j in range(tk // LANES)]
            if masked:
                chunks = [jnp.where(col + j * LANES <= row, ch, NEG_INF) for j, ch in enumerate(chunks)]
            mx = chunks[0]
            for ch in chunks[1:]:
                mx = jnp.maximum(mx, ch)
            m_prev = m_ref[c]
            m_new = jnp.maximum(m_prev, jnp.max(mx, axis=1, keepdims=True))
            alpha = jnp.exp2(m_prev - m_new)
            es = [jnp.exp2(ch - m_new) for ch in chunks]
            tot = es[0]
            for e in es[1:]:
                tot = tot + e
            l_ref[c] = alpha * l_ref[c] + tot
            acc_ref[c] = alpha * acc_ref[c] + _dot(jnp.concatenate(es, axis=1).astype(BF16), vb)
            m_ref[c] = m_new

    @pl.when(ki < ratio * qi)
    def _():
        step(False)

    @pl.when(ki >= ratio * qi)
    def _():
        step(True)

    @pl.when(ki == ratio * (qi + 1) - 1)
    def _():
        lam = _lam_value(lam_ref, lam_init)
        l0 = jnp.sum(l_ref[0], axis=1, keepdims=True)
        l1 = jnp.sum(l_ref[1], axis=1, keepdims=True)
        o = acc_ref[0] / l0 - lam * (acc_ref[1] / l1)
        o_ref[...] = _rms(o) * sub_ref[...] * (1.0 - lam_init)


def _attn_prompt(qkvu, lam_vec, subln, lam_init):
    m = qkvu.shape[0]
    tk = min(ATT_TK, m)
    tq = min(ATT_TQ, m)
    assert m % tq == 0 and tq % tk == 0
    ratio = tq // tk
    pairs = [(qi, ki) for qi in range(m // tq) for ki in range(ratio * (qi + 1))]
    qi_tab = jnp.asarray([p[0] for p in pairs], jnp.int32)
    ki_tab = jnp.asarray([p[1] for p in pairs], jnp.int32)
    kcol = QK_WIDTH // LANES
    vcol = 2 * QK_WIDTH // LANES
    grid_spec = pltpu.PrefetchScalarGridSpec(
        num_scalar_prefetch=2,
        grid=(A_HEADS, len(pairs)),
        in_specs=[
            pl.BlockSpec((tq, LANES), lambda h, p, qt, kt: (qt[p], h)),
            pl.BlockSpec((tk, LANES), lambda h, p, qt, kt: (kt[p], kcol + h)),
            pl.BlockSpec((tk, LANES), lambda h, p, qt, kt: (kt[p], vcol + h)),
            pl.BlockSpec((4, A_QK_DIM), lambda h, p, qt, kt: (0, 0)),
            pl.BlockSpec((1, A_V_DIM), lambda h, p, qt, kt: (0, 0)),
        ],
        out_specs=pl.BlockSpec((tq, LANES), lambda h, p, qt, kt: (qt[p], h)),
        scratch_shapes=[pltpu.VMEM((2, tq, LANES), F32)] * 3,
    )
    return pl.pallas_call(
        functools.partial(_attn_p_kernel, lam_init=lam_init),
        grid_spec=grid_spec,
        out_shape=jax.ShapeDtypeStruct((m, A_WIDTH), F32),
        compiler_params=_params(("arbitrary", "arbitrary")),
        name="attn_prompt",
    )(qi_tab, ki_tab, qkvu, qkvu, qkvu, lam_vec, subln.reshape(1, A_V_DIM))


def _attn_s_kernel(pt_ref, q_ref, kn_ref, vn_ref, lam_ref, sub_ref, rep_ref, *rest, pps, lam_init):
    kt_refs = rest[:pps]
    v_refs = rest[pps:2 * pps]
    o_ref, m_ref, l_ref, acc_ref = rest[2 * pps:]
    g = pl.program_id(1)
    nrow = 2 * A_HEADS
    row = lax.broadcasted_iota(jnp.int32, (nrow, QK_WIDTH), 0)
    lane = lax.broadcasted_iota(jnp.int32, (nrow, QK_WIDTH), 1)
    own = (lane // A_QK_DIM) == row
    qmat = jnp.where(own, q_ref[...] * A_SCALE, 0.0).astype(BF16)
    page = kt_refs[0].shape[1]

    @pl.when(g == 0)
    def _():
        kn = jnp.where(own, kn_ref[...], 0.0).astype(BF16)
        m_ref[...] = jnp.sum(qmat.astype(F32) * kn.astype(F32), axis=1, keepdims=True)
        l_ref[...] = jnp.ones(l_ref.shape, F32)
        vn = vn_ref[...].astype(BF16).astype(F32)
        for h in range(A_HEADS):
            acc_ref[2 * h:2 * h + 2, :] = jnp.broadcast_to(vn[:, h * A_V_DIM:(h + 1) * A_V_DIM], (2, A_V_DIM))

    s = jnp.concatenate([_dot(qmat, kt_refs[i][...].astype(BF16)) for i in range(pps)], axis=1)
    m_prev = m_ref[...]
    m_new = jnp.maximum(m_prev, jnp.max(s, axis=1, keepdims=True))
    alpha = jnp.exp(m_prev - m_new)
    e = jnp.exp(s - m_new)
    l_ref[...] = alpha * l_ref[...] + jnp.sum(e, axis=1, keepdims=True)
    m_ref[...] = m_new
    eb = jnp.concatenate([e[:, i * page:(i + 1) * page] for i in range(pps)], axis=0).astype(BF16)
    spread = _dot(eb, rep_ref[...])
    vrow = lax.broadcasted_iota(jnp.int32, (nrow, page * A_HEADS), 0)
    vlane = lax.broadcasted_iota(jnp.int32, (nrow, page * A_HEADS), 1)
    mine = (vlane % A_HEADS) == (vrow // 2)
    w = jnp.concatenate([jnp.where(mine, spread[i * nrow:(i + 1) * nrow], 0.0).astype(BF16) for i in range(pps)],
                        axis=1)
    v_all = jnp.concatenate([v_refs[i][...].astype(BF16) for i in range(pps)], axis=0)
    acc_ref[...] = alpha * acc_ref[...] + _dot(w, v_all)

    @pl.when(g == pl.num_programs(1) - 1)
    def _():
        lam = _lam_value(lam_ref, lam_init)
        r = acc_ref[...] / l_ref[...]
        for h in range(A_HEADS):
            o = r[2 * h:2 * h + 1] - lam * r[2 * h + 1:2 * h + 2]
            o_ref[:, h * A_V_DIM:(h + 1) * A_V_DIM] = _rms(o) * sub_ref[...] * (1.0 - lam_init)


def _attn_decode(q, k_new, v_new, cache_k, cache_v, layer, page_table, lam_vec, subln, lam_init):
    b = q.shape[0]
    n_pages = page_table.shape[1]
    pps = math.gcd(DEC_PPS, n_pages)
    page = cache_k.shape[3]
    row_spec = pl.BlockSpec((None, 1, QK_WIDTH), lambda bi, g, pt: (bi, 0, 0))
    rep = (jnp.arange(page * A_HEADS)[None, :] // A_HEADS == jnp.arange(page)[:, None]).astype(BF16)

    def page_spec(i, rows, cols):
        return pl.BlockSpec((None, None, rows, cols),
                            lambda bi, g, pt: (layer, pt[bi * n_pages + g * pps + i], 0, 0))

    grid_spec = pltpu.PrefetchScalarGridSpec(
        num_scalar_prefetch=1,
        grid=(b, n_pages // pps),
        in_specs=[row_spec, row_spec, row_spec,
                  pl.BlockSpec((4, A_QK_DIM), lambda bi, g, pt: (0, 0)),
                  pl.BlockSpec((1, A_V_DIM), lambda bi, g, pt: (0, 0)),
                  pl.BlockSpec((page, page * A_HEADS), lambda bi, g, pt: (0, 0))]
                 + [page_spec(i, QK_WIDTH, page) for i in range(pps)]
                 + [page_spec(i, page * A_HEADS, A_V_DIM) for i in range(pps)],
        out_specs=row_spec,
        scratch_shapes=[pltpu.VMEM((2 * A_HEADS, 1), F32), pltpu.VMEM((2 * A_HEADS, 1), F32),
                        pltpu.VMEM((2 * A_HEADS, A_V_DIM), F32)],
    )
    return pl.pallas_call(
        functools.partial(_attn_s_kernel, pps=pps, lam_init=lam_init),
        grid_spec=grid_spec,
        out_shape=jax.ShapeDtypeStruct((b, 1, A_WIDTH), F32),
        compiler_params=_params(("arbitrary", "arbitrary")),
        name="attn_decode",
    )(page_table.reshape(-1), q, k_new, v_new, lam_vec, subln.reshape(1, A_V_DIM), rep,
      *([cache_k] * pps), *([cache_v] * pps))


def _pool_project(pooled, pw_ref, ps_ref, o_ref):
    for g in range(len(POOL_WINDOWS)):
        cs = slice(g * POOL_GROUP_DIM, (g + 1) * POOL_GROUP_DIM)
        o_ref[:, cs] = _dot(pooled[g].astype(BF16), pw_ref[g].astype(BF16)) * ps_ref[:, cs]


def _pool_p_kernel(u_ref, halo_ref, pw_ref, ps_ref, o_ref, s_ref):
    i = pl.program_id(0)
    tm = u_ref.shape[0]
    hr = halo_ref.shape[0]
    s_ref[0:hr] = jnp.where(i > 0, halo_ref[...], 0.0)
    s_ref[hr:] = u_ref[...]
    rows = hr + tm
    d = 1
    while d < POOL_WINDOWS[-1]:
        c0 = sum(POOL_GROUP_DIM for w in POOL_WINDOWS if w <= d)
        s_ref[d:rows, c0:] = s_ref[d:rows, c0:] + s_ref[0:rows - d, c0:]
        d *= 2
    pos = i * tm + lax.broadcasted_iota(jnp.int32, (tm, 1), 0)
    pooled = []
    for g, w in enumerate(POOL_WINDOWS):
        cs = slice(g * POOL_GROUP_DIM, (g + 1) * POOL_GROUP_DIM)
        cnt = jnp.minimum(w, pos + 1).astype(F32)
        pooled.append(s_ref[hr:, cs] / cnt - u_ref[:, cs])
    _pool_project(pooled, pw_ref, ps_ref, o_ref)


def _pool_prompt(qkvu, pool_w, pool_scale):
    m = qkvu.shape[0]
    tm = min(TM, m)
    hr = 2 * SUBLANES
    ucol = (2 * QK_WIDTH + A_WIDTH) // POOL_WIDTH
    return pl.pallas_call(
        _pool_p_kernel,
        grid=(m // tm,),
        in_specs=[
            pl.BlockSpec((tm, POOL_WIDTH), lambda i: (i, ucol)),
            pl.BlockSpec((hr, POOL_WIDTH), lambda i: (jnp.maximum(i * (tm // hr) - 1, 0), ucol)),
            pl.BlockSpec(pool_w.shape, lambda i: (0, 0, 0)),
            pl.BlockSpec((1, POOL_WIDTH), lambda i: (0, 0)),
        ],
        out_specs=pl.BlockSpec((tm, POOL_WIDTH), lambda i: (i, 0)),
        out_shape=jax.ShapeDtypeStruct((m, POOL_WIDTH), F32),
        scratch_shapes=[pltpu.VMEM((hr + tm, POOL_WIDTH), F32)],
        compiler_params=_params(("arbitrary",)),
        name="pool_prompt",
    )(qkvu, qkvu, pool_w, pool_scale.reshape(1, POOL_WIDTH))


def _pool_s_kernel(u_ref, hist_ref, pw_ref, ps_ref, o_ref):
    u = u_ref[...]
    pooled = []
    for g, w in enumerate(POOL_WINDOWS):
        cs = slice(g * POOL_GROUP_DIM, (g + 1) * POOL_GROUP_DIM)
        tot = u[:, cs]
        for k in range(1, w):
            tot = tot + hist_ref[POOL_HIST - k, :, cs]
        pooled.append(tot / float(w) - u[:, cs])
    _pool_project(pooled, pw_ref, ps_ref, o_ref)


def _pool_decode(u, hist_t, pool_w, pool_scale):
    b = u.shape[0]
    return pl.pallas_call(
        _pool_s_kernel,
        grid=(1,),
        in_specs=[
            pl.BlockSpec(u.shape, lambda i: (0, 0)),
            pl.BlockSpec(hist_t.shape, lambda i: (0, 0, 0)),
            pl.BlockSpec(pool_w.shape, lambda i: (0, 0, 0)),
            pl.BlockSpec((1, POOL_WIDTH), lambda i: (0, 0)),
        ],
        out_specs=pl.BlockSpec((b, POOL_WIDTH), lambda i: (0, 0)),
        out_shape=jax.ShapeDtypeStruct((b, POOL_WIDTH), F32),
        compiler_params=_params(("arbitrary",)),
        name="pool_decode",
    )(u, hist_t, pool_w, pool_scale.reshape(1, POOL_WIDTH))


def _softplus(x):
    return jnp.maximum(x, 0.0) + jnp.log1p(jnp.exp(-jnp.abs(x)))


def _conv_silu(raw_ref, tail_ref, ext_ref, w_ref, b_ref):
    q = raw_ref.shape[0]
    t = tail_ref.shape[0]
    ext_ref[0:t] = tail_ref[...]
    ext_ref[t:] = raw_ref[...]
    tail_ref[...] = raw_ref[q - t:q]
    out = b_ref[...]
    for tau in range(SSM_CONV):
        off = t - (SSM_CONV - 1) + tau
        out = out + ext_ref[off:off + q] * w_ref[tau:tau + 1]
    return _silu(out)


def _ssd_p_kernel(x_ref, b_ref, c_ref, z_ref, dt_ref, cwx_ref, cwb_ref, cwc_ref, cbx_ref, cbb_ref, cbc_ref,
                  pr_ref, dsk_ref, nw_ref, y_ref, hout_ref,
                  h_ref, tx_ref, tb_ref, tc_ref, ex_ref, eb_ref, ec_ref):
    c = pl.program_id(1)
    q = x_ref.shape[0]
    gw = x_ref.shape[1]

    @pl.when(c == 0)
    def _():
        h_ref[...] = jnp.zeros(h_ref.shape, F32)
        tx_ref[...] = jnp.zeros(tx_ref.shape, F32)
        tb_ref[...] = jnp.zeros(tb_ref.shape, F32)
        tc_ref[...] = jnp.zeros(tc_ref.shape, F32)

    xs = _conv_silu(x_ref, tx_ref, ex_ref, cwx_ref, cbx_ref)
    bm = _conv_silu(b_ref, tb_ref, eb_ref, cwb_ref, cbb_ref)
    cm = _conv_silu(c_ref, tc_ref, ec_ref, cwc_ref, cbc_ref)

    dt = _softplus(dt_ref[...] + pr_ref[0:1])
    a = -jnp.exp(pr_ref[1:2])
    dta = dt * a

    ri = lax.broadcasted_iota(jnp.int32, (q, q), 0)
    ci = lax.broadcasted_iota(jnp.int32, (q, q), 1)
    causal = ci <= ri
    tri = causal.astype(BF16)
    hi, mid, lo = _split3(dta)
    cum = _dot(tri, hi) + _dot(tri, mid) + _dot(tri, lo)
    cum_t = cum.T

    er = lax.broadcasted_iota(jnp.int32, (LANES, gw), 0)
    el = lax.broadcasted_iota(jnp.int32, (LANES, gw), 1)
    expand = (el // SSM_HEAD_DIM == er).astype(BF16)
    dt_e = _dot_exact_rhs(dt, expand)
    cum_e = _dot_exact_rhs(cum, expand)
    exp_cum_e = jnp.exp(cum_e)
    cum_last_e = cum_e[q - 1:q]
    decay_end_e = jnp.exp(cum_last_e - cum_e)

    xdt = xs * dt_e
    xdt_b = xdt.astype(BF16)
    bm_b = bm.astype(BF16)
    cm_b = cm.astype(BF16)
    cb = _dot_nt(cm_b, bm_b)

    lane = lax.broadcasted_iota(jnp.int32, (q, LANES), 1)
    y_parts = []
    for pp in range(SSM_HPG // 2):
        xp = xdt_b[:, pp * LANES:(pp + 1) * LANES]
        ys = []
        for r in (2 * pp, 2 * pp + 1):
            seg = cum[:, r:r + 1] - cum_t[r:r + 1, :]
            lmat = jnp.exp(jnp.where(causal, seg, NEG_INF))
            ys.append(_dot((cb * lmat).astype(BF16), xp))
        y_parts.append(jnp.where(lane < SSM_HEAD_DIM, ys[0], ys[1]))
    y_diag = jnp.concatenate(y_parts, axis=1)

    h_prev = h_ref[...]
    y_off = _dot(cm_b, h_prev.astype(BF16)) * exp_cum_e
    states = _dot(bm.T.astype(BF16), (xdt * decay_end_e).astype(BF16))
    h_new = h_prev * jnp.exp(cum_last_e) + states
    h_ref[...] = h_new

    y = y_diag + y_off + dsk_ref[...] * xs
    y = y * _silu(z_ref[...])
    y_ref[...] = _rms(y) * nw_ref[...]

    @pl.when(c == pl.num_programs(1) - 1)
    def _():
        hout_ref[...] = h_new


def _ssd_prompt(zx, conv_w, conv_b, params, d_skip_e, norm_w):
    m = zx.shape[0]
    q = min(SSM_CHUNK, m)
    gw = SSM_GROUP_W
    xcol = SSM_INNER // gw
    bcol = 2 * SSM_INNER // SSM_STATE
    ccol = bcol + SSM_GROUPS
    dcol = ccol + SSM_GROUPS
    tail = SUBLANES
    cw2 = conv_w
    cb2 = conv_b.reshape(1, -1)
    in_specs = [
        pl.BlockSpec((q, gw), lambda g, c: (c, xcol + g)),
        pl.BlockSpec((q, SSM_STATE), lambda g, c: (c, bcol + g)),
        pl.BlockSpec((q, SSM_STATE), lambda g, c: (c, ccol + g)),
        pl.BlockSpec((q, gw), lambda g, c: (c, g)),
        pl.BlockSpec((q, LANES), lambda g, c: (c, dcol + g)),
        pl.BlockSpec((SSM_CONV, gw), lambda g, c: (0, g)),
        pl.BlockSpec((SSM_CONV, SSM_STATE), lambda g, c: (0, SSM_INNER // SSM_STATE + g)),
        pl.BlockSpec((SSM_CONV, SSM_STATE), lambda g, c: (0, SSM_INNER // SSM_STATE + SSM_GROUPS + g)),
        pl.BlockSpec((1, gw), lambda g, c: (0, g)),
        pl.BlockSpec((1, SSM_STATE), lambda g, c: (0, SSM_INNER // SSM_STATE + g)),
        pl.BlockSpec((1, SSM_STATE), lambda g, c: (0, SSM_INNER // SSM_STATE + SSM_GROUPS + g)),
        pl.BlockSpec((None, SUBLANES, LANES), lambda g, c: (g, 0, 0)),
        pl.BlockSpec((1, gw), lambda g, c: (0, g)),
        pl.BlockSpec((1, gw), lambda g, c: (0, g)),
    ]
    return pl.pallas_call(
        _ssd_p_kernel,
        grid=(SSM_GROUPS, m // q),
        in_specs=in_specs,
        out_specs=[pl.BlockSpec((q, gw), lambda g, c: (c, g)),
                   pl.BlockSpec((None, SSM_STATE, gw), lambda g, c: (g, 0, 0))],
        out_shape=[jax.ShapeDtypeStruct((m, SSM_INNER), F32),
                   jax.ShapeDtypeStruct((SSM_GROUPS, SSM_STATE, gw), F32)],
        scratch_shapes=[pltpu.VMEM((SSM_STATE, gw), F32),
                        pltpu.VMEM((tail, gw), F32), pltpu.VMEM((tail, SSM_STATE), F32),
                        pltpu.VMEM((tail, SSM_STATE), F32),
                        pltpu.VMEM((tail + q, gw), F32), pltpu.VMEM((tail + q, SSM_STATE), F32),
                        pltpu.VMEM((tail + q, SSM_STATE), F32)],
        compiler_params=_params(("arbitrary", "arbitrary")),
        name="ssd_prompt",
    )(zx, zx, zx, zx, zx, cw2, cw2, cw2, cb2, cb2, cb2, params, d_skip_e.reshape(1, -1), norm_w.reshape(1, -1))


def _ssd_s_pre_kernel(xbc_ref, hist_ref, cw_ref, cb_ref, dt_ref, dtb_ref, alog_ref, xs_ref, bc_ref, xdt_ref, dec_ref):
    out = cb_ref[...] + xbc_ref[...] * cw_ref[SSM_CONV - 1:SSM_CONV]
    for tau in range(SSM_CONV - 1):
        out = out + hist_ref[tau] * cw_ref[tau:tau + 1]
    act = _silu(out)
    xs = act[:, :SSM_INNER]
    dt = _softplus(dt_ref[...] + dtb_ref[...])
    xs_ref[...] = xs
    bc_ref[...] = act[:, SSM_INNER:]
    xdt_ref[...] = xs * dt
    dec_ref[...] = jnp.exp(dt * -jnp.exp(alog_ref[...]))


def _ssd_decode_pre(xbc, hist_t, conv_w, conv_b, dt_e, dtb_e, alog_e):
    b = xbc.shape[0]
    full = lambda a: pl.BlockSpec(a.shape, lambda i: (0,) * a.ndim)
    args = (xbc, hist_t, conv_w, conv_b.reshape(1, -1), dt_e, dtb_e.reshape(1, -1), alog_e.reshape(1, -1))
    shapes = [jax.ShapeDtypeStruct((b, SSM_INNER), F32), jax.ShapeDtypeStruct((b, 2 * SSM_BC_W), F32),
              jax.ShapeDtypeStruct((b, SSM_INNER), F32), jax.ShapeDtypeStruct((b, SSM_INNER), F32)]
    return pl.pallas_call(
        _ssd_s_pre_kernel,
        grid=(1,),
        in_specs=[full(a) for a in args],
        out_specs=[pl.BlockSpec(s.shape, lambda i: (0, 0)) for s in shapes],
        out_shape=shapes,
        compiler_params=_params(("arbitrary",)),
        name="ssd_decode_pre",
    )(*args)


def _ssd_s_state_kernel(h_ref, xdt_ref, dec_ref, b_ref, c_ref, ho_ref, y_ref):
    nblk = xdt_ref.shape[1]
    rows_per_group = SSM_GROUP_W
    lane = lax.broadcasted_iota(jnp.int32, y_ref.shape, 1)
    y = jnp.zeros(y_ref.shape, F32)
    for blk in range(nblk):
        g = (blk * LANES) // rows_per_group
        rs = slice(blk * LANES, (blk + 1) * LANES)
        h_new = (h_ref[rs, :] * dec_ref[:, blk:blk + 1]
                 + xdt_ref[:, blk:blk + 1] * b_ref[:, g * SSM_STATE:(g + 1) * SSM_STATE])
        ho_ref[rs, :] = h_new
        col = jnp.sum(h_new * c_ref[:, g * SSM_STATE:(g + 1) * SSM_STATE], axis=1, keepdims=True)
        y = jnp.where(lane == blk, col, y)
    y_ref[...] = y


def _ssd_decode_state(h0, xdt_t, dec_t, bm, cm):
    b, hp, n = h0.shape
    nblk = hp // LANES
    return pl.pallas_call(
        _ssd_s_state_kernel,
        grid=(b,),
        in_specs=[
            pl.BlockSpec((None, hp, n), lambda i: (i, 0, 0)),
            pl.BlockSpec((None, LANES, nblk), lambda i: (i, 0, 0)),
            pl.BlockSpec((None, LANES, nblk), lambda i: (i, 0, 0)),
            pl.BlockSpec((None, 1, SSM_BC_W), lambda i: (i, 0, 0)),
            pl.BlockSpec((None, 1, SSM_BC_W), lambda i: (i, 0, 0)),
        ],
        out_specs=[pl.BlockSpec((None, hp, n), lambda i: (i, 0, 0)),
                   pl.BlockSpec((None, LANES, nblk), lambda i: (i, 0, 0))],
        out_shape=[jax.ShapeDtypeStruct((b, hp, n), F32), jax.ShapeDtypeStruct((b, LANES, nblk), F32)],
        compiler_params=_params(("arbitrary",)),
        name="ssd_decode_state",
    )(h0, xdt_t, dec_t, bm, cm)


def _ssd_s_post_kernel(y_ref, xs_ref, z_ref, dsk_ref, nw_ref, o_ref):
    y = (y_ref[...] + dsk_ref[...] * xs_ref[...]) * _silu(z_ref[...])
    for g in range(SSM_GROUPS):
        cs = slice(g * SSM_GROUP_W, (g + 1) * SSM_GROUP_W)
        o_ref[:, cs] = _rms(y[:, cs]) * nw_ref[:, cs]


def _ssd_decode_post(y, xs, z, d_skip_e, norm_w):
    args = (y, xs, z, d_skip_e.reshape(1, -1), norm_w.reshape(1, -1))
    return pl.pallas_call(
        _ssd_s_post_kernel,
        grid=(1,),
        in_specs=[pl.BlockSpec(a.shape, lambda i: (0, 0)) for a in args],
        out_specs=pl.BlockSpec(y.shape, lambda i: (0, 0)),
        out_shape=jax.ShapeDtypeStruct(y.shape, F32),
        compiler_params=_params(("arbitrary",)),
        name="ssd_decode_post",
    )(*args)


def _route_rows(x, sc, sh, wr_ref, br_ref, hn_ref, ti_ref, tp_ref):
    rows = x.shape[0]
    hn = _rms(x) * (1.0 + sc) + sh
    hn_ref[0:rows] = hn
    ah, am, al = _split3(hn)
    wh, wm, wl = _split3(wr_ref[...])
    lg = (_dot(ah, wh) + _dot(ah, wm) + _dot(am, wh) + _dot(ah, wl) + _dot(al, wh) + _dot(am, wm)) + br_ref[...]
    lane = lax.broadcasted_iota(jnp.int32, lg.shape, 1)
    lg = jnp.where(lane < N_EXPERTS, lg, NEG_INF)
    vals, idxs = [], []
    for _ in range(TOP_K):
        mx = jnp.max(lg, axis=1, keepdims=True)
        idx = jnp.min(jnp.where(lg == mx, lane, LANES), axis=1, keepdims=True)
        vals.append(mx)
        idxs.append(idx)
        lg = jnp.where(lane == idx, NEG_INF, lg)
    es = [jnp.exp(v - vals[0]) for v in vals]
    tot = es[0] + es[1] + es[2] + es[3]
    ti = jnp.zeros(lg.shape, jnp.int32)
    tp = jnp.zeros(lg.shape, F32)
    for k in range(TOP_K):
        ti = jnp.where(lane == k, idxs[k], ti)
        tp = jnp.where(lane == k, es[k] / tot, tp)
    ti_ref[0:rows] = ti
    tp_ref[0:rows] = tp


def _router_kernel(xp_ref, scp_ref, shp_ref, xs_ref, scs_ref, shs_ref, wr_ref, br_ref, hn_ref, ti_ref, tp_ref):
    last = pl.num_programs(0) - 1

    @pl.when(pl.program_id(0) < last)
    def _():
        _route_rows(xp_ref[...], scp_ref[...], shp_ref[...], wr_ref, br_ref, hn_ref, ti_ref, tp_ref)

    @pl.when(pl.program_id(0) == last)
    def _():
        _route_rows(xs_ref[...], scs_ref[...], shs_ref[...], wr_ref, br_ref, hn_ref, ti_ref, tp_ref)


def _norm_router(xp, sc_p, sh_p, xs, sc_s, sh_s, wr_pad, br_pad):
    mp, d = xp.shape
    nb = xs.shape[0]
    tm = min(TM, mp)
    assert mp % tm == 0 and nb <= tm
    np_ = mp // tm
    t_all = mp + nb
    row = pl.BlockSpec((1, d), lambda i: (0, 0))
    dec = pl.BlockSpec((nb, d), lambda i: (0, 0))
    return pl.pallas_call(
        _router_kernel,
        grid=(np_ + 1,),
        in_specs=[pl.BlockSpec((tm, d), lambda i: (jnp.minimum(i, np_ - 1), 0)), row, row, dec, dec, dec,
                  pl.BlockSpec((d, LANES), lambda i: (0, 0)), pl.BlockSpec((1, LANES), lambda i: (0, 0))],
        out_specs=[pl.BlockSpec((tm, d), lambda i: (i, 0)),
                   pl.BlockSpec((tm, LANES), lambda i: (i, 0)),
                   pl.BlockSpec((tm, LANES), lambda i: (i, 0))],
        out_shape=[jax.ShapeDtypeStruct((t_all, d), F32),
                   jax.ShapeDtypeStruct((t_all, LANES), jnp.int32),
                   jax.ShapeDtypeStruct((t_all, LANES), F32)],
        compiler_params=_params(("arbitrary",)),
        name="norm_router",
    )(xp, sc_p, sh_p, xs, sc_s, sh_s, wr_pad, br_pad)


def _route_plan(top_i, tm, n_tiles):
    t = top_i.shape[0]
    e_flat = top_i.reshape(-1)
    onehot = (e_flat[:, None] == jnp.arange(N_EXPERTS, dtype=jnp.int32)[None, :]).astype(jnp.int32)
    csum = jnp.cumsum(onehot, axis=0)
    rank = jnp.take_along_axis(csum, e_flat[:, None], axis=1)[:, 0] - 1
    cnt = csum[-1]
    tiles_e = (cnt + tm - 1) // tm
    tiles_end = jnp.cumsum(tiles_e)
    tile_start = tiles_end - tiles_e
    dest = (tile_start[e_flat] * tm + rank).astype(jnp.int32)
    total = tiles_end[-1]
    tile_ids = jnp.arange(n_tiles, dtype=jnp.int32)
    tile_valid = (tile_ids < total).astype(jnp.int32)
    tile_expert = jnp.sum((tiles_end[None, :] <= tile_ids[:, None]).astype(jnp.int32), axis=1)
    tile_expert = jnp.minimum(tile_expert, N_EXPERTS - 1)
    last_expert = tile_expert[jnp.maximum(total - 1, 0)]
    tile_expert = jnp.where(tile_valid == 1, tile_expert, last_expert)
    tile_src = jnp.minimum(tile_ids, jnp.maximum(total - 1, 0)).astype(jnp.int32)
    return dest, tile_expert.astype(jnp.int32), tile_valid, tile_src


def _dispatch_kernel(dest_ref, hn_ref, xs_in, xs_hbm, sem, *, tc):
    del xs_in
    base = pl.program_id(0) * tc

    def body(j, carry):
        for k in range(TOP_K):
            row = dest_ref[(base + j) * TOP_K + k]
            pltpu.make_async_copy(hn_ref.at[pl.ds(j, 1)], xs_hbm.at[pl.ds(row, 1)], sem).start()
        return carry

    lax.fori_loop(0, tc, body, 0)
    for _ in range(TOP_K):
        pltpu.make_async_copy(hn_ref, xs_hbm.at[pl.ds(0, tc)], sem).wait()


def _dispatch_rows(t):
    best = None
    for cand in range(SUBLANES, min(t, DISPATCH_MAX_ROWS) + 1, SUBLANES):
        if t % cand == 0:
            best = cand
    assert best is not None, "token count must be a multiple of the sublane count"
    return best


def _moe_dispatch(dest, hn_all, n_rows):
    t, w = hn_all.shape
    tc = _dispatch_rows(t)
    grid_spec = pltpu.PrefetchScalarGridSpec(
        num_scalar_prefetch=1,
        grid=(t // tc,),
        in_specs=[pl.BlockSpec((tc, w), lambda i, dr: (i, 0)), pl.BlockSpec(memory_space=pl.ANY)],
        out_specs=pl.BlockSpec(memory_space=pl.ANY),
        scratch_shapes=[pltpu.SemaphoreType.DMA(())],
    )
    return pl.pallas_call(
        functools.partial(_dispatch_kernel, tc=tc),
        grid_spec=grid_spec,
        out_shape=jax.ShapeDtypeStruct((n_rows, w), hn_all.dtype),
        input_output_aliases={2: 0},
        compiler_params=_params(("arbitrary",)),
        name="moe_dispatch",
    )(dest, hn_all, jnp.zeros((n_rows, w), hn_all.dtype))


def _moe_kernel(te_ref, tv_ref, ts_ref, xs_ref, sel_ref, wgu_ref, bgu_ref, wdn_ref, bdn_ref, o_ref, xb_ref, *, tn):
    r = pl.program_id(0)
    c = pl.program_id(1)
    del te_ref, ts_ref

    @pl.when((c == 0) & (tv_ref[r] == 0))
    def _():
        o_ref[...] = jnp.zeros(o_ref.shape, F32)

    @pl.when((c == 0) & (tv_ref[r] == 1))
    def _():
        xb_ref[...] = xs_ref[...].astype(BF16)
        o_ref[...] = jnp.broadcast_to(bdn_ref[...], o_ref.shape)

    @pl.when(tv_ref[r] == 1)
    def _():
        sw = sel_ref.shape[0]
        gu = _dot(xb_ref[...], wgu_ref[...].astype(BF16)) + bgu_ref[...]
        up = jnp.concatenate([pltpu.roll(gu[:, j * LANES:(j + 1) * LANES], LANES - 1, 1)
                              for j in range(tn // LANES)], axis=1)
        gate = jnp.minimum(gu, SWIGLU_LIMIT)
        up = jnp.clip(up, -SWIGLU_LIMIT, SWIGLU_LIMIT)
        act = ((up + 1.0) * gate * jax.nn.sigmoid(SWIGLU_ALPHA * gate)).astype(BF16)
        parts = [_dot(act[:, k * sw:(k + 1) * sw], sel_ref[...]).astype(BF16) for k in range(tn // sw)]
        o_ref[...] += _dot(jnp.concatenate(parts, axis=1), wdn_ref[...].astype(BF16))


def _moe_tm(n_tokens):
    fair = -(-(n_tokens * TOP_K) // N_EXPERTS)
    return 16 * (-(-(fair * 11) // (16 * 16)))


def _moe_experts(xs, plan, w_gu, b_gu, w_dn, b_dn, layer, tm, n_tiles):
    _, tile_expert, tile_valid, tile_src = plan
    d = w_dn.shape[-1]
    f2 = w_gu.shape[-1]
    tn = MOE_TN
    ne = w_gu.shape[1]
    sw = MOE_SEL_W
    sel = (jnp.arange(sw)[:, None] == 2 * jnp.arange(sw // 2)[None, :]).astype(BF16)
    grid_spec = pltpu.PrefetchScalarGridSpec(
        num_scalar_prefetch=3,
        grid=(n_tiles, f2 // tn),
        in_specs=[
            pl.BlockSpec((tm, d), lambda r, c, te, tv, ts: (ts[r], 0)),
            pl.BlockSpec((sw, sw // 2), lambda r, c, te, tv, ts: (0, 0)),
            pl.BlockSpec((None, None, d, tn), lambda r, c, te, tv, ts: (layer, te[r], 0, c)),
            pl.BlockSpec((None, None, 1, tn), lambda r, c, te, tv, ts: (layer, te[r], 0, c)),
            pl.BlockSpec((None, None, tn // 2, d), lambda r, c, te, tv, ts: (layer, te[r], c, 0)),
            pl.BlockSpec((None, None, 1, d), lambda r, c, te, tv, ts: (layer, te[r], 0, 0)),
        ],
        out_specs=pl.BlockSpec((tm, d), lambda r, c, te, tv, ts: (r, 0)),
        scratch_shapes=[pltpu.VMEM((tm, d), BF16)],
    )
    return pl.pallas_call(
        functools.partial(_moe_kernel, tn=tn),
        grid_spec=grid_spec,
        out_shape=jax.ShapeDtypeStruct((n_tiles * tm, d), F32),
        compiler_params=_params(("arbitrary", "arbitrary")),
        name="moe_experts",
    )(tile_expert, tile_valid, tile_src, xs, sel, w_gu,
      b_gu.reshape(b_gu.shape[0], ne, 1, f2), w_dn, b_dn.reshape(b_dn.shape[0], ne, 1, d))


def _combine_kernel(dest_ref, ys_hbm, tp_ref, x_ref, g_ref, fw_ref, o_ref, yb_ref, sem, *, tc, tok0, final):
    i = pl.program_id(0)
    nsteps = pl.num_programs(0)

    def start_gather(tile, s):
        base = (tok0 + tile * tc) * TOP_K

        def body(j, carry):
            for k in range(TOP_K):
                row = dest_ref[base + j * TOP_K + k]
                pltpu.make_async_copy(ys_hbm.at[pl.ds(row, 1)], yb_ref.at[s, k, pl.ds(j, 1)], sem.at[s]).start()
            return carry

        lax.fori_loop(0, tc, body, 0)

    slot = i % 2

    @pl.when(i == 0)
    def _():
        start_gather(0, 0)

    for k in range(TOP_K):
        pltpu.make_async_copy(ys_hbm.at[pl.ds(0, tc)], yb_ref.at[slot, k], sem.at[slot]).wait()

    @pl.when(i + 1 < nsteps)
    def _():
        start_gather(i + 1, 1 - slot)

    tp = tp_ref[...]
    acc = yb_ref[slot, 0] * tp[:, 0:1]
    for k in range(1, TOP_K):
        acc = acc + yb_ref[slot, k] * tp[:, k:k + 1]
    out = x_ref[...] + g_ref[...] * acc
    if final:
        out = _rms(out) * fw_ref[...]
    o_ref[...] = out


def _moe_combine(dest, ys, top_p, x, gate, final_w, tok0, final):
    m, d = x.shape
    tc = min(256, m)
    mg = gate.shape[0]
    blk0 = tok0 // tc
    grid_spec = pltpu.PrefetchScalarGridSpec(
        num_scalar_prefetch=1,
        grid=(m // tc,),
        in_specs=[
            pl.BlockSpec(memory_space=pl.ANY),
            pl.BlockSpec((tc, LANES), lambda i, dr: (blk0 + i, 0)),
            pl.BlockSpec((tc, d), lambda i, dr: (i, 0)),
            pl.BlockSpec((tc if mg > 1 else 1, d), (lambda i, dr: (i, 0)) if mg > 1 else (lambda i, dr: (0, 0))),
            pl.BlockSpec((1, d), lambda i, dr: (0, 0)),
        ],
        out_specs=pl.BlockSpec((tc, d), lambda i, dr: (i, 0)),
        scratch_shapes=[pltpu.VMEM((2, TOP_K, tc, d), F32), pltpu.SemaphoreType.DMA((2,))],
    )
    return pl.pallas_call(
        functools.partial(_combine_kernel, tc=tc, tok0=tok0, final=final),
        grid_spec=grid_spec,
        out_shape=jax.ShapeDtypeStruct((m, d), F32),
        compiler_params=_params(("arbitrary",)),
        name="moe_combine",
    )(dest, ys, top_p, x, gate, final_w.reshape(1, d))


def _rope_tables(pos):
    half = A_QK_DIM // 2
    inv = ROPE_THETA ** (-jnp.arange(half, dtype=F32) * 2.0 / A_QK_DIM)
    ang = pos.astype(F32)[:, None] * inv[None, :]
    cos, sin = jnp.cos(ang), jnp.sin(ang)
    cos_t = jnp.tile(cos, (1, LANES // half))
    sin_t = jnp.tile(jnp.concatenate([-sin, sin], axis=1), (1, LANES // A_QK_DIM))
    return cos_t, sin_t


def kernel(x_prompt, x_sample, cache_k, cache_v, state_pool, state_conv, state_ssm, page_table, c_prompt, c_sample,
           ada_w, ada_b, ab_w_in, ab_lambda, ab_subln, ab_pool_w, ab_pool_scale, ab_w_out,
           ssd_w_in, ssd_conv_w, ssd_conv_b, ssd_dt_bias, ssd_a_log, ssd_d, ssd_norm, ssd_w_out,
           router_w, router_b, moe_w_gu, moe_b_gu, moe_w_dn, moe_b_dn, final_norm):
    d = D_MODEL
    seq = x_prompt.shape[1]
    nb = x_sample.shape[0]
    n_pages = page_table.shape[1]
    past_len = n_pages * cache_k.shape[2]
    t_all = seq + nb
    moe_tm = _moe_tm(t_all)
    n_tiles = -(-(t_all * TOP_K) // moe_tm) + N_EXPERTS

    c_rows = SUBLANES * (-(-(1 + nb) // SUBLANES))
    c_all = jnp.concatenate([c_prompt, c_sample, jnp.zeros((c_rows - 1 - nb, d), F32)], axis=0)
    mod = _ada_mod(c_all, ada_w, ada_b).reshape(DEPTH, c_rows, N_MOD, d)

    def mods(layer, m):
        return mod[layer, 0:1, m], mod[layer, 1:1 + nb, m]

    xp = x_prompt.reshape(seq, d)
    xs_ = x_sample.reshape(nb, d)
    rope_p = _rope_tables(jnp.arange(seq))
    rope_s = _rope_tables(jnp.full((nb,), past_len))
    nl_c, n_phys, page = cache_k.shape[:3]
    cache_k4 = jnp.transpose(cache_k, (0, 1, 3, 4, 5, 2)).reshape(nl_c, n_phys, QK_WIDTH, page)
    cache_v4 = cache_v.reshape(nl_c, n_phys, page * A_HEADS, A_V_DIM)

    def moe_block(layer, xp, xs_, final):
        sh_p, sh_s = mods(layer, 3)
        sc_p, sc_s = mods(layer, 4)
        g_p, g_s = mods(layer, 5)
        wr = jnp.pad(router_w[layer], ((0, 0), (0, LANES - N_EXPERTS)))
        br = jnp.pad(router_b[layer], (0, LANES - N_EXPERTS)).reshape(1, LANES)
        hn_all, ti_all, tp_all = _norm_router(xp, sc_p, sh_p, xs_, sc_s, sh_s, wr, br)
        plan = _route_plan(ti_all[:, :TOP_K], moe_tm, n_tiles)
        xs_sorted = _moe_dispatch(plan[0], hn_all, n_tiles * moe_tm)
        ys = _moe_experts(xs_sorted, plan, moe_w_gu, moe_b_gu, moe_w_dn, moe_b_dn, layer, moe_tm, n_tiles)
        xp = _moe_combine(plan[0], ys, tp_all, xp, g_p, final_norm, 0, final)
        xs_ = _moe_combine(plan[0], ys, tp_all, xs_, g_s, final_norm, seq, final)
        return xp, xs_

    sh_p, sh_s = mods(0, 0)
    sc_p, sc_s = mods(0, 1)
    g_p, g_s = mods(0, 2)
    lam0 = _lambda_init(0)
    w_in0 = ab_w_in[0].astype(BF16)
    w_out0 = ab_w_out[0].astype(BF16)
    ucol = 2 * QK_WIDTH + A_WIDTH

    qkvu_p = _norm_matmul(xp, sc_p, sh_p, w_in0, 1024, rope_p, 2 * QK_WIDTH)
    o_p = _attn_prompt(qkvu_p, ab_lambda[0], ab_subln[0], lam0)
    pg_p = _pool_prompt(qkvu_p, ab_pool_w[0], ab_pool_scale[0])
    k_prompt = qkvu_p[:, QK_WIDTH:2 * QK_WIDTH].reshape(1, 1, seq, A_HEADS, 2, A_QK_DIM)
    v_prompt = qkvu_p[:, 2 * QK_WIDTH:ucol].reshape(1, 1, seq, A_HEADS, A_V_DIM)
    pool_prompt = qkvu_p[seq - POOL_HIST:, ucol:].reshape(1, 1, POOL_HIST, POOL_WIDTH)
    xp = _matmul_resid([o_p, pg_p], w_out0, xp, g_p)

    qkvu_s = _norm_matmul(xs_, sc_s, sh_s, w_in0, 1024, rope_s, 2 * QK_WIDTH)
    q_s = qkvu_s[:, :QK_WIDTH].reshape(nb, 1, QK_WIDTH)
    k_s = qkvu_s[:, QK_WIDTH:2 * QK_WIDTH].reshape(nb, 1, QK_WIDTH)
    v_s = qkvu_s[:, 2 * QK_WIDTH:ucol].reshape(nb, 1, A_WIDTH)
    u_s = qkvu_s[:, ucol:]
    o_s = _attn_decode(q_s, k_s, v_s, cache_k4, cache_v4, 0, page_table, ab_lambda[0], ab_subln[0], lam0)
    pg_s = _pool_decode(u_s, jnp.swapaxes(state_pool[0], 0, 1), ab_pool_w[0], ab_pool_scale[0])
    k_sample = k_s.reshape(1, nb, 1, A_HEADS, 2, A_QK_DIM)
    v_sample = v_s.reshape(1, nb, 1, A_HEADS, A_V_DIM)
    pool_sample = jnp.concatenate([state_pool[0][:, 1:], u_s[:, None, :]], axis=1)[None]
    xs_ = _matmul_resid([o_s.reshape(nb, A_WIDTH), pg_s], w_out0, xs_, g_s)

    xp, xs_ = moe_block(0, xp, xs_, False)

    sh_p, sh_s = mods(1, 0)
    sc_p, sc_s = mods(1, 1)
    g_p, g_s = mods(1, 2)
    zx_w = SSM_INNER + SSM_CONV_DIM
    w_dt = ssd_w_in[0][:, zx_w:].reshape(d, SSM_GROUPS, SSM_HPG)
    w_dt = jnp.pad(w_dt, ((0, 0), (0, 0), (0, LANES - SSM_HPG))).reshape(d, SSM_GROUPS * LANES)
    w_in1 = jnp.concatenate([ssd_w_in[0][:, :zx_w], w_dt], axis=1).astype(BF16)
    w_out1 = ssd_w_out[0].astype(BF16)
    tn1 = w_in1.shape[1] // SSM_GROUPS

    def per_group(v):
        return jnp.pad(v.reshape(SSM_GROUPS, SSM_HPG), ((0, 0), (0, LANES - SSM_HPG)))

    zeros_g = jnp.zeros((SSM_GROUPS, LANES), F32)
    params = jnp.stack([per_group(ssd_dt_bias[0]), per_group(ssd_a_log[0])] + [zeros_g] * (SUBLANES - 2), axis=1)
    dsk_e = jnp.repeat(ssd_d[0], SSM_HEAD_DIM)
    dtb_e = jnp.repeat(ssd_dt_bias[0], SSM_HEAD_DIM)
    alog_e = jnp.repeat(ssd_a_log[0], SSM_HEAD_DIM)

    zx_p = _norm_matmul(xp, sc_p, sh_p, w_in1, tn1)
    y_p, h_p = _ssd_prompt(zx_p, ssd_conv_w[0], ssd_conv_b[0], params, dsk_e, ssd_norm[0])
    conv_prompt = zx_p[seq - (SSM_CONV - 1):, SSM_INNER:zx_w].reshape(1, 1, SSM_CONV - 1, SSM_CONV_DIM)
    ssm_prompt = jnp.transpose(h_p.reshape(SSM_GROUPS, SSM_STATE, SSM_HPG, SSM_HEAD_DIM), (0, 2, 3, 1))
    ssm_prompt = ssm_prompt.reshape(1, 1, SSM_HEADS, SSM_HEAD_DIM, SSM_STATE)
    xp = _matmul_resid([y_p], w_out1, xp, g_p)

    zx_s = _norm_matmul(xs_, sc_s, sh_s, w_in1, tn1)
    z_s = zx_s[:, :SSM_INNER]
    xbc_s = zx_s[:, SSM_INNER:zx_w]
    dt_s = zx_s[:, zx_w:].reshape(nb, SSM_GROUPS, LANES)[:, :, :SSM_HPG].reshape(nb, SSM_HEADS)
    dt_e = jnp.repeat(dt_s, SSM_HEAD_DIM, axis=1)
    xs_act, bc_s, xdt_s, dec_s = _ssd_decode_pre(xbc_s, jnp.swapaxes(state_conv[0], 0, 1), ssd_conv_w[0],
                                                 ssd_conv_b[0], dt_e, dtb_e, alog_e)
    nblk = SSM_INNER // LANES

    def col_major(v):
        return jnp.swapaxes(v.reshape(nb, nblk, LANES), 1, 2)

    h_s, y_t = _ssd_decode_state(state_ssm[0].reshape(nb, SSM_INNER, SSM_STATE), col_major(xdt_s), col_major(dec_s),
                                 bc_s[:, :SSM_BC_W].reshape(nb, 1, SSM_BC_W), bc_s[:, SSM_BC_W:].reshape(nb, 1, SSM_BC_W))
    y_s = jnp.swapaxes(y_t, 1, 2).reshape(nb, SSM_INNER)
    yn_s = _ssd_decode_post(y_s, xs_act, z_s, dsk_e, ssd_norm[0])
    conv_sample = jnp.concatenate([state_conv[0][:, 1:], xbc_s[:, None, :]], axis=1)[None]
    ssm_sample = h_s.reshape(1, nb, SSM_HEADS, SSM_HEAD_DIM, SSM_STATE)
    xs_ = _matmul_resid([yn_s], w_out1, xs_, g_s)

    xp, xs_ = moe_block(1, xp, xs_, True)

    return (xp.reshape(1, seq, d), xs_.reshape(nb, 1, d), k_prompt, v_prompt, pool_prompt, conv_prompt, ssm_prompt,
            k_sample, v_sample, pool_sample, conv_sample, ssm_sample)
```
